```python
import jax, jax.numpy as jnp
from jax import lax
import numpy as np

D_MODEL = 1024
BATCH = 8
SEQ = 4096
DEPTH = 2

D_MIX = D_MODEL
CONV_CH = D_MIX // 2
CONV_WIDTH = 31
FOX_HEADS = 8
FOX_HEAD_DIM = 64
FOX_WIDTH = FOX_HEADS * FOX_HEAD_DIM
Q_BLOCK = 128
N_MEM = 256
MEM_HEADS = 4
MEM_HEAD_DIM = 128
MEM_INNER = MEM_HEADS * MEM_HEAD_DIM
D_FF = 4 * D_MODEL
EPS = 1e-6
NEG_INF = -1e30
IN_COLS = 2 * CONV_CH + 3 * FOX_WIDTH + FOX_HEADS

kernel_name = "hymba_conformer_fox_sandwich_memory"


def rms_norm(x, g):
    xf = x.astype(jnp.float32)
    y = xf * lax.rsqrt(jnp.mean(xf * xf, axis=-1, keepdims=True) + EPS)
    return (y * g.astype(jnp.float32)).astype(x.dtype)


def layer_norm(x, g, b):
    xf = x.astype(jnp.float32)
    mu = jnp.mean(xf, axis=-1, keepdims=True)
    xc = xf - mu
    y = xc * lax.rsqrt(jnp.mean(xc * xc, axis=-1, keepdims=True) + EPS)
    return (y * g.astype(jnp.float32) + b.astype(jnp.float32)).astype(x.dtype)


def causal_depthwise_conv(u, w, b):
    out = lax.conv_general_dilated(
        u, w[:, None, :], window_strides=(1,), padding=[(CONV_WIDTH - 1, 0)],
        dimension_numbers=("NWC", "WIO", "NWC"), feature_group_count=u.shape[-1])
    return out + b


def forgetting_attention(q, k, v, log_f):
    S = q.shape[1]
    dh = q.shape[-1]
    scale = dh ** -0.5
    cum = jnp.cumsum(log_f, axis=1).transpose(0, 2, 1)
    outs = []
    for i in range(S // Q_BLOCK):
        q0 = i * Q_BLOCK
        kl = q0 + Q_BLOCK
        qb = q[:, q0:kl]
        kb = k[:, :kl]
        vb = v[:, :kl]
        logits = jnp.einsum("bqhd,bkhd->bhqk", qb, kb,
                            preferred_element_type=jnp.float32) * scale
        logits = logits + cum[:, :, q0:kl, None] - cum[:, :, None, :kl]
        q_pos = q0 + jnp.arange(Q_BLOCK)
        k_pos = jnp.arange(kl)
        mask = k_pos[None, :] <= q_pos[:, None]
        p = jax.nn.softmax(jnp.where(mask, logits, NEG_INF), axis=-1)
        outs.append(jnp.einsum("bhqk,bkhd->bqhd", p.astype(vb.dtype), vb))
    return jnp.concatenate(outs, axis=1)


def hybrid_mixer(h, w_in, b_forget, conv_w, conv_b, conv_ln_g, conv_ln_b, w_out):
    B, S, _ = h.shape
    z = h @ w_in
    o = 0
    a = z[..., o:o + CONV_CH]; o += CONV_CH
    g = z[..., o:o + CONV_CH]; o += CONV_CH
    q = z[..., o:o + FOX_WIDTH]; o += FOX_WIDTH
    k = z[..., o:o + FOX_WIDTH]; o += FOX_WIDTH
    v = z[..., o:o + FOX_WIDTH]; o += FOX_WIDTH
    f_logit = z[..., o:o + FOX_HEADS]

    u = a * jax.nn.sigmoid(g)
    u = causal_depthwise_conv(u, conv_w, conv_b)
    u = jax.nn.silu(layer_norm(u, conv_ln_g, conv_ln_b))

    log_f = jax.nn.log_sigmoid((f_logit + b_forget).astype(jnp.float32))
    shp = (B, S, FOX_HEADS, FOX_HEAD_DIM)
    att = forgetting_attention(q.reshape(shp), k.reshape(shp), v.reshape(shp), log_f)
    att = att.reshape(B, S, FOX_WIDTH)

    return jnp.concatenate([u, att], axis=-1) @ w_out


def memory_cross_attention(h, mem_n, w_mq, w_mk, w_mv, w_mo):
    B, S, _ = h.shape
    q = (h @ w_mq).reshape(B, S, MEM_HEADS, MEM_HEAD_DIM)
    k = (mem_n @ w_mk).reshape(B, N_MEM, MEM_HEADS, MEM_HEAD_DIM)
    v = (mem_n @ w_mv).reshape(B, N_MEM, MEM_HEADS, MEM_HEAD_DIM)
    logits = jnp.einsum("bqhd,bmhd->bhqm", q, k,
                        preferred_element_type=jnp.float32) * (MEM_HEAD_DIM ** -0.5)
    p = jax.nn.softmax(logits, axis=-1)
    out = jnp.einsum("bhqm,bmhd->bqhd", p.astype(v.dtype), v).reshape(B, S, MEM_INNER)
    return out @ w_mo


def squared_relu_mlp(h, w_up, w_down):
    return jnp.square(jax.nn.relu(h @ w_up)) @ w_down


def setup_inputs(seed: int = 0) -> dict:
    key = jax.random.key(seed)
    ks = jax.random.split(key, 24)
    nrm = lambda k, shape, fan_in: jax.random.normal(k, shape, jnp.float32) * (fan_in ** -0.5)
    gain = lambda k, shape: 1.0 + 0.05 * jax.random.normal(k, shape, jnp.float32)
    small = lambda k, shape: 0.02 * jax.random.normal(k, shape, jnp.float32)
    L = DEPTH
    return {
        "x": jax.random.normal(ks[0], (BATCH, SEQ, D_MODEL), jnp.float32),
        "mem": jax.random.normal(ks[1], (BATCH, N_MEM, D_MODEL), jnp.float32),
        "norm_mix_pre": gain(ks[2], (L, D_MODEL)),
        "norm_mix_post": gain(ks[3], (L, D_MODEL)),
        "w_in": nrm(ks[4], (L, D_MODEL, IN_COLS), D_MODEL),
        "b_forget": jax.random.uniform(ks[5], (L, FOX_HEADS), jnp.float32, 1.0, 5.0),
        "conv_w": nrm(ks[6], (L, CONV_WIDTH, CONV_CH), CONV_WIDTH),
        "conv_b": small(ks[7], (L, CONV_CH)),
        "conv_ln_g": gain(ks[8], (L, CONV_CH)),
        "conv_ln_b": small(ks[9], (L, CONV_CH)),
        "w_out": nrm(ks[10], (L, D_MIX, D_MODEL), D_MIX),
        "norm_mem_pre": gain(ks[11], (L, D_MODEL)),
        "norm_mem_post": gain(ks[12], (L, D_MODEL)),
        "norm_memkv": gain(ks[13], (L, D_MODEL)),
        "w_mq": nrm(ks[14], (L, D_MODEL, MEM_INNER), D_MODEL),
        "w_mk": nrm(ks[15], (L, D_MODEL, MEM_INNER), D_MODEL),
        "w_mv": nrm(ks[16], (L, D_MODEL, MEM_INNER), D_MODEL),
        "w_mo": nrm(ks[17], (L, MEM_INNER, D_MODEL), MEM_INNER),
        "norm_mlp_pre": gain(ks[18], (L, D_MODEL)),
        "norm_mlp_post": gain(ks[19], (L, D_MODEL)),
        "w_up": nrm(ks[20], (L, D_MODEL, D_FF), D_MODEL),
        "w_down": nrm(ks[21], (L, D_FF, D_MODEL), D_FF),
    }


def reference(x, mem, norm_mix_pre, norm_mix_post, w_in, b_forget, conv_w, conv_b,
              conv_ln_g, conv_ln_b, w_out, norm_mem_pre, norm_mem_post, norm_memkv,
              w_mq, w_mk, w_mv, w_mo, norm_mlp_pre, norm_mlp_post, w_up, w_down):
    for l in range(DEPTH):
        h = rms_norm(x, norm_mix_pre[l])
        y = hybrid_mixer(h, w_in[l], b_forget[l], conv_w[l], conv_b[l],
                         conv_ln_g[l], conv_ln_b[l], w_out[l])
        x = x + rms_norm(y, norm_mix_post[l])
        h = rms_norm(x, norm_mem_pre[l])
        mem_n = rms_norm(mem, norm_memkv[l])
        y = memory_cross_attention(h, mem_n, w_mq[l], w_mk[l], w_mv[l], w_mo[l])
        x = x + rms_norm(y, norm_mem_post[l])
        h = rms_norm(x, norm_mlp_pre[l])
        y = squared_relu_mlp(h, w_up[l], w_down[l])
        x = x + rms_norm(y, norm_mlp_post[l])
    return x
```

```python
import functools

import jax
import jax.numpy as jnp
from jax import lax
from jax.experimental import pallas as pl
from jax.experimental.pallas import tpu as pltpu

D_MODEL = 1024
CONV_CH = 512
CONV_WIDTH = 31
FOX_HEADS = 8
FOX_HEAD_DIM = 64
FOX_WIDTH = FOX_HEADS * FOX_HEAD_DIM
N_MEM = 256
MEM_HEADS = 4
MEM_HEAD_DIM = 128
MEM_INNER = MEM_HEADS * MEM_HEAD_DIM
D_FF = 4 * D_MODEL
EPS = 1e-6
NEG_INF = -1e30

LANES = 128
SUBLANES = 8
MIB = 1024 * 1024

TM_IN = 512
TM_OUT = 256
CONV_HALO = 32
CONV_ROWS = 32
TQ = 256
TK = 256
HEAD_GROUP = 4
GROUP_W = HEAD_GROUP * FOX_HEAD_DIM
TM_MEM = 512
TM_MLP = 512
FF_CHUNK = 1024

BF16 = jnp.bfloat16
F32 = jnp.float32


def _rms(x, gain):
    return x * lax.rsqrt(jnp.mean(x * x, axis=-1, keepdims=True) + EPS) * gain


def _sigmoid(x):
    return 1.0 / (1.0 + jnp.exp(-x))


def _log_sigmoid(x):
    return -(jnp.maximum(-x, 0.0) + jnp.log1p(jnp.exp(-jnp.abs(x))))


def _split3(x):
    hi = x.astype(BF16)
    r1 = x - hi.astype(F32)
    mid = r1.astype(BF16)
    lo = (r1 - mid.astype(F32)).astype(BF16)
    return hi, mid, lo


def _mixer_in_kernel(tiles_per_seq, x_ref, g_ref, wag_ref, wqkv_ref, wf_ref, bf_ref, tri_ref,
                     u_ref, q_ref, k_ref, v_ref, ccol_ref, crow_ref, carry_ref):
    i = pl.program_id(0)

    @pl.when(i % tiles_per_seq == 0)
    def _():
        carry_ref[...] = jnp.zeros_like(carry_ref)

    h = _rms(x_ref[...], g_ref[...]).astype(BF16)
    ag = jnp.dot(h, wag_ref[...], preferred_element_type=F32)
    u_ref[...] = ag[:, :CONV_CH] * _sigmoid(ag[:, CONV_CH:])
    qkv = jnp.dot(h, wqkv_ref[...], preferred_element_type=F32)
    q_ref[...] = (qkv[:, :FOX_WIDTH] * (FOX_HEAD_DIM ** -0.5)).astype(BF16)
    k_ref[...] = qkv[:, FOX_WIDTH:2 * FOX_WIDTH].astype(BF16)
    v_ref[...] = qkv[:, 2 * FOX_WIDTH:].astype(BF16)
    f = jnp.dot(h, wf_ref[...], preferred_element_type=F32) + bf_ref[...]
    logf = _log_sigmoid(f)
    tri = tri_ref[...]
    hi, mid, lo = _split3(logf)
    cum = (jnp.dot(tri, hi, preferred_element_type=F32)
           + jnp.dot(tri, mid, preferred_element_type=F32)
           + jnp.dot(tri, lo, preferred_element_type=F32)) + carry_ref[0:1, :]
    carry_ref[...] = jnp.broadcast_to(cum[TM_IN - 1:TM_IN, :], carry_ref.shape)
    ccol_ref[...] = cum
    crow_ref[...] = cum.T[:FOX_HEADS, :]


def _mixer_in(x2, gain, wag, wqkv, wf, bfp, tri, seq):
    t = x2.shape[0]
    const = lambda i: (0, 0)
    row = lambda i: (i, 0)
    return pl.pallas_call(
        functools.partial(_mixer_in_kernel, seq // TM_IN),
        grid=(t // TM_IN,),
        in_specs=[
            pl.BlockSpec((TM_IN, D_MODEL), row),
            pl.BlockSpec((1, D_MODEL), const),
            pl.BlockSpec((D_MODEL, 2 * CONV_CH), const),
            pl.BlockSpec((D_MODEL, 3 * FOX_WIDTH), const),
            pl.BlockSpec((D_MODEL, LANES), const),
            pl.BlockSpec((1, LANES), const),
            pl.BlockSpec((TM_IN, TM_IN), const),
        ],
        out_specs=[
            pl.BlockSpec((TM_IN, CONV_CH), row),
            pl.BlockSpec((TM_IN, FOX_WIDTH), row),
            pl.BlockSpec((TM_IN, FOX_WIDTH), row),
            pl.BlockSpec((TM_IN, FOX_WIDTH), row),
            pl.BlockSpec((TM_IN, LANES), row),
            pl.BlockSpec((FOX_HEADS, TM_IN), lambda i: (0, i)),
        ],
        out_shape=[
            jax.ShapeDtypeStruct((t, CONV_CH), F32),
            jax.ShapeDtypeStruct((t, FOX_WIDTH), BF16),
            jax.ShapeDtypeStruct((t, FOX_WIDTH), BF16),
            jax.ShapeDtypeStruct((t, FOX_WIDTH), BF16),
            jax.ShapeDtypeStruct((t, LANES), F32),
            jax.ShapeDtypeStruct((FOX_HEADS, t), F32),
        ],
        scratch_shapes=[pltpu.VMEM((SUBLANES, LANES), F32)],
        compiler_params=pltpu.CompilerParams(
            dimension_semantics=("arbitrary",), vmem_limit_bytes=48 * MIB),
        name="mixer_in",
    )(x2, gain, wag, wqkv, wf, bfp, tri)


def _fox_attn_kernel(q_ref, k_ref, v_ref, ccol_ref, crow_ref, o_ref,
                     qcat_ref, s_ref, mx_ref, r_ref, l_ref, acc_ref):
    i = pl.program_id(1)
    n_kv = i + 1
    lane_head = lax.broadcasted_iota(jnp.int32, (TQ, GROUP_W), 1) // FOX_HEAD_DIM
    row_pos = lax.broadcasted_iota(jnp.int32, (TQ, TK), 0)
    col_pos = lax.broadcasted_iota(jnp.int32, (TQ, TK), 1)
    ccol = ccol_ref[...]

    for g in range(FOX_HEADS // HEAD_GROUP):
        lanes = slice(g * GROUP_W, (g + 1) * GROUP_W)
        qg = q_ref[:, lanes].astype(F32)
        for h in range(HEAD_GROUP):
            qcat_ref[h * TQ:(h + 1) * TQ, :] = jnp.where(lane_head == h, qg, 0.0).astype(BF16)
            mx_ref[h] = jnp.full((TQ, LANES), NEG_INF, F32)

        def pass1(j, carry):
            ks = pl.multiple_of(j * TK, TK)
            kj = k_ref[pl.ds(ks, TK), lanes]
            s = lax.dot_general(qcat_ref[...], kj, (((1,), (1,)), ((), ())),
                                preferred_element_type=F32)
            visible = (col_pos + j * TK) <= (row_pos + i * TQ)
            for h in range(HEAD_GROUP):
                hg = g * HEAD_GROUP + h
                ck = crow_ref[hg:hg + 1, pl.ds(ks, TK)]
                t = jnp.where(visible, s[h * TQ:(h + 1) * TQ, :] - ck, NEG_INF)
                s_ref[h, :, pl.ds(ks, TK)] = t
                m = mx_ref[h]
                for c in range(TK // LANES):
                    m = jnp.maximum(m, t[:, c * LANES:(c + 1) * LANES])
                mx_ref[h] = m
            return carry

        lax.fori_loop(0, n_kv, pass1, 0)

        for h in range(HEAD_GROUP):
            hg = g * HEAD_GROUP + h
            cq = ccol[:, hg:hg + 1]
            m = jnp.max(mx_ref[h], axis=-1, keepdims=True) + cq
            r_ref[h] = jnp.broadcast_to(cq - m, (TQ, LANES))
            l_ref[h] = jnp.zeros((TQ, LANES), F32)
            acc_ref[h] = jnp.zeros((TQ, FOX_HEAD_DIM), F32)

        def pass2(j, carry):
            ks = pl.multiple_of(j * TK, TK)
            for h in range(HEAD_GROUP):
                hg = g * HEAD_GROUP + h
                r = r_ref[h]
                lsum = l_ref[h]
                ps = []
                for c in range(TK // LANES):
                    p = jnp.exp(s_ref[h, :, pl.ds(ks + c * LANES, LANES)] + r)
                    lsum = lsum + p
                    ps.append(p.astype(BF16))
                l_ref[h] = lsum
                vj = v_ref[pl.ds(ks, TK), hg * FOX_HEAD_DIM:(hg + 1) * FOX_HEAD_DIM]
                acc_ref[h] += jnp.dot(jnp.concatenate(ps, axis=1), vj,
                                      preferred_element_type=F32)
            return carry

        lax.fori_loop(0, n_kv, pass2, 0)

        outs = []
        for h in range(HEAD_GROUP):
            l = jnp.sum(l_ref[h], axis=-1, keepdims=True)
            outs.append(acc_ref[h] / l)
        o_ref[:, lanes] = jnp.concatenate(outs, axis=1).astype(BF16)


def _fox_attn(q, k, v, ccol, crow, batch, seq):
    t = q.shape[0]
    nq = seq // TQ
    return pl.pallas_call(
        _fox_attn_kernel,
        grid=(batch, nq),
        in_specs=[
            pl.BlockSpec((TQ, FOX_WIDTH), lambda b, i: (b * nq + i, 0)),
            pl.BlockSpec((seq, FOX_WIDTH), lambda b, i: (b, 0)),
            pl.BlockSpec((seq, FOX_WIDTH), lambda b, i: (b, 0)),
            pl.BlockSpec((TQ, LANES), lambda b, i: (b * nq + i, 0)),
            pl.BlockSpec((FOX_HEADS, seq), lambda b, i: (0, b)),
        ],
        out_specs=pl.BlockSpec((TQ, FOX_WIDTH), lambda b, i: (b * nq + i, 0)),
        out_shape=jax.ShapeDtypeStruct((t, FOX_WIDTH), BF16),
        scratch_shapes=[
            pltpu.VMEM((HEAD_GROUP * TQ, GROUP_W), BF16),
            pltpu.VMEM((HEAD_GROUP, TQ, seq), F32),
            pltpu.VMEM((HEAD_GROUP, TQ, LANES), F32),
            pltpu.VMEM((HEAD_GROUP, TQ, LANES), F32),
            pltpu.VMEM((HEAD_GROUP, TQ, LANES), F32),
            pltpu.VMEM((HEAD_GROUP, TQ, FOX_HEAD_DIM), F32),
        ],
        compiler_params=pltpu.CompilerParams(
            dimension_semantics=("arbitrary", "arbitrary"), vmem_limit_bytes=56 * MIB),
        name="fox_attn",
    )(q, k, v, ccol, crow)


def _mixer_out_kernel(tiles_per_seq, x_ref, u_ref, uprev_ref, att_ref, cw_ref, cb_ref, lng_ref,
                      lnb_ref, wo_ref, gpost_ref, o_ref, ubuf_ref, cat_ref):
    i = pl.program_id(0)
    first = (i % tiles_per_seq == 0)
    ubuf_ref[0:CONV_HALO, :] = jnp.where(first, 0.0, uprev_ref[...])
    ubuf_ref[CONV_HALO:, :] = u_ref[...]
    base = CONV_HALO - (CONV_WIDTH - 1)
    for c in range(TM_OUT // CONV_ROWS):
        r0 = c * CONV_ROWS
        acc = jnp.broadcast_to(cb_ref[...], (CONV_ROWS, CONV_CH))
        for kk in range(CONV_WIDTH):
            acc = acc + cw_ref[kk:kk + 1, :] * ubuf_ref[r0 + base + kk:r0 + base + kk + CONV_ROWS, :]
        mu = jnp.mean(acc, axis=-1, keepdims=True)
        xc = acc - mu
        y = xc * lax.rsqrt(jnp.mean(xc * xc, axis=-1, keepdims=True) + EPS)
        y = y * lng_ref[...] + lnb_ref[...]
        cat_ref[r0:r0 + CONV_ROWS, 0:CONV_CH] = (y * _sigmoid(y)).astype(BF16)
    cat_ref[:, CONV_CH:] = att_ref[...]
    y = jnp.dot(cat_ref[...], wo_ref[...], preferred_element_type=F32)
    o_ref[...] = x_ref[...] + _rms(y, gpost_ref[...])


def _mixer_out(x2, u, att, cw, cb, lng, lnb, wo, gpost, seq):
    t = x2.shape[0]
    const = lambda i: (0, 0)
    row = lambda i: (i, 0)
    halo_blocks = TM_OUT // CONV_HALO
    return pl.pallas_call(
        functools.partial(_mixer_out_kernel, seq // TM_OUT),
        grid=(t // TM_OUT,),
        in_specs=[
            pl.BlockSpec((TM_OUT, D_MODEL), row),
            pl.BlockSpec((TM_OUT, CONV_CH), row),
            pl.BlockSpec((CONV_HALO, CONV_CH), lambda i: (jnp.maximum(i * halo_blocks - 1, 0), 0)),
            pl.BlockSpec((TM_OUT, FOX_WIDTH), row),
            pl.BlockSpec((CONV_WIDTH, CONV_CH), const),
            pl.BlockSpec((1, CONV_CH), const),
            pl.BlockSpec((1, CONV_CH), const),
            pl.BlockSpec((1, CONV_CH), const),
            pl.BlockSpec((CONV_CH + FOX_WIDTH, D_MODEL), const),
            pl.BlockSpec((1, D_MODEL), const),
        ],
        out_specs=pl.BlockSpec((TM_OUT, D_MODEL), row),
        out_shape=jax.ShapeDtypeStruct((t, D_MODEL), F32),
        scratch_shapes=[
            pltpu.VMEM((CONV_HALO + TM_OUT, CONV_CH), F32),
            pltpu.VMEM((TM_OUT, CONV_CH + FOX_WIDTH), BF16),
        ],
        compiler_params=pltpu.CompilerParams(
            dimension_semantics=("arbitrary",), vmem_limit_bytes=32 * MIB),
        name="mixer_out",
    )(x2, u, u, att, cw, cb, lng, lnb, wo, gpost)


def _mem_kv_kernel(mem_ref, g_ref, wk_ref, wv_ref, k_ref, v_ref):
    mn = _rms(mem_ref[...], g_ref[...]).astype(BF16)
    k_ref[...] = jnp.dot(mn, wk_ref[...], preferred_element_type=F32).astype(BF16)
    v_ref[...] = jnp.dot(mn, wv_ref[...], preferred_element_type=F32).astype(BF16)


def _mem_kv(mem2, gain, wk, wv, batch):
    const = lambda b: (0, 0)
    row = lambda b: (b, 0)
    return pl.pallas_call(
        _mem_kv_kernel,
        grid=(batch,),
        in_specs=[
            pl.BlockSpec((N_MEM, D_MODEL), row),
            pl.BlockSpec((1, D_MODEL), const),
            pl.BlockSpec((D_MODEL, MEM_INNER), const),
            pl.BlockSpec((D_MODEL, MEM_INNER), const),
        ],
        out_specs=[pl.BlockSpec((N_MEM, MEM_INNER), row), pl.BlockSpec((N_MEM, MEM_INNER), row)],
        out_shape=[jax.ShapeDtypeStruct((batch * N_MEM, MEM_INNER), BF16)] * 2,
        compiler_params=pltpu.CompilerParams(dimension_semantics=("arbitrary",)),
        name="mem_kv",
    )(mem2, gain, wk, wv)


def _mem_attn_kernel(x_ref, gpre_ref, wq_ref, km_ref, vm_ref, wo_ref, gpost_ref, o_ref, cat_ref):
    x = x_ref[...]
    h = _rms(x, gpre_ref[...]).astype(BF16)
    q = jnp.dot(h, wq_ref[...], preferred_element_type=F32)
    for hd in range(MEM_HEADS):
        lanes = slice(hd * MEM_HEAD_DIM, (hd + 1) * MEM_HEAD_DIM)
        s = lax.dot_general(q[:, lanes].astype(BF16), km_ref[:, lanes], (((1,), (1,)), ((), ())),
                            preferred_element_type=F32) * (MEM_HEAD_DIM ** -0.5)
        p = jnp.exp(s - jnp.max(s, axis=-1, keepdims=True))
        l = jnp.sum(p, axis=-1, keepdims=True)
        o = jnp.dot(p.astype(BF16), vm_ref[:, lanes], preferred_element_type=F32)
        cat_ref[:, lanes] = (o / l).astype(BF16)
    y = jnp.dot(cat_ref[...], wo_ref[...], preferred_element_type=F32)
    o_ref[...] = x + _rms(y, gpost_ref[...])


def _mem_attn(x2, gpre, wq, km, vm, wo, gpost, seq):
    t = x2.shape[0]
    tiles_per_seq = seq // TM_MEM
    const = lambda i: (0, 0)
    row = lambda i: (i, 0)
    per_batch = lambda i: (i // tiles_per_seq, 0)
    return pl.pallas_call(
        _mem_attn_kernel,
        grid=(t // TM_MEM,),
        in_specs=[
            pl.BlockSpec((TM_MEM, D_MODEL), row),
            pl.BlockSpec((1, D_MODEL), const),
            pl.BlockSpec((D_MODEL, MEM_INNER), const),
            pl.BlockSpec((N_MEM, MEM_INNER), per_batch),
            pl.BlockSpec((N_MEM, MEM_INNER), per_batch),
            pl.BlockSpec((MEM_INNER, D_MODEL), const),
            pl.BlockSpec((1, D_MODEL), const),
        ],
        out_specs=pl.BlockSpec((TM_MEM, D_MODEL), row),
        out_shape=jax.ShapeDtypeStruct((t, D_MODEL), F32),
        scratch_shapes=[pltpu.VMEM((TM_MEM, MEM_INNER), BF16)],
        compiler_params=pltpu.CompilerParams(
            dimension_semantics=("arbitrary",), vmem_limit_bytes=32 * MIB),
        name="mem_attn",
    )(x2, gpre, wq, km, vm, wo, gpost)


def _mlp_kernel(x_ref, gpre_ref, wup_ref, wdn_ref, gpost_ref, o_ref):
    x = x_ref[...]
    h = _rms(x, gpre_ref[...]).astype(BF16)
    y = jnp.zeros((TM_MLP, D_MODEL), F32)
    for c in range(D_FF // FF_CHUNK):
        cols = slice(c * FF_CHUNK, (c + 1) * FF_CHUNK)
        a = jnp.maximum(jnp.dot(h, wup_ref[:, cols], preferred_element_type=F32), 0.0)
        y = y + jnp.dot((a * a).astype(BF16), wdn_ref[cols, :], preferred_element_type=F32)
    o_ref[...] = x + _rms(y, gpost_ref[...])


def _mlp(x2, gpre, wup, wdn, gpost):
    t = x2.shape[0]
    const = lambda i: (0, 0)
    row = lambda i: (i, 0)
    return pl.pallas_call(
        _mlp_kernel,
        grid=(t // TM_MLP,),
        in_specs=[
            pl.BlockSpec((TM_MLP, D_MODEL), row),
            pl.BlockSpec((1, D_MODEL), const),
            pl.BlockSpec((D_MODEL, D_FF), const),
            pl.BlockSpec((D_FF, D_MODEL), const),
            pl.BlockSpec((1, D_MODEL), const),
        ],
        out_specs=pl.BlockSpec((TM_MLP, D_MODEL), row),
        out_shape=jax.ShapeDtypeStruct((t, D_MODEL), F32),
        compiler_params=pltpu.CompilerParams(
            dimension_semantics=("arbitrary",), vmem_limit_bytes=56 * MIB),
        name="mlp",
    )(x2, gpre, wup, wdn, gpost)


def kernel(x, mem, norm_mix_pre, norm_mix_post, w_in, b_forget, conv_w, conv_b, conv_ln_g, conv_ln_b, w_out, norm_mem_pre, norm_mem_post, norm_memkv, w_mq, w_mk, w_mv, w_mo, norm_mlp_pre, norm_mlp_post, w_up, w_down):
    batch, seq, d = x.shape
    depth = w_in.shape[0]
    assert d == D_MODEL and seq % TM_IN == 0 and seq % TQ == 0 and mem.shape[1] == N_MEM
    x2 = x.reshape(batch * seq, d)
    mem2 = mem.reshape(batch * N_MEM, d)
    tri = (jnp.arange(TM_IN)[:, None] >= jnp.arange(TM_IN)[None, :]).astype(BF16)
    row = lambda a: a.reshape(1, -1)
    f_pad = LANES - FOX_HEADS
    for l in range(depth):
        wag = w_in[l, :, :2 * CONV_CH].astype(BF16)
        wqkv = w_in[l, :, 2 * CONV_CH:2 * CONV_CH + 3 * FOX_WIDTH].astype(BF16)
        wf = jnp.pad(w_in[l, :, 2 * CONV_CH + 3 * FOX_WIDTH:], ((0, 0), (0, f_pad))).astype(BF16)
        bfp = jnp.pad(b_forget[l], (0, f_pad)).reshape(1, LANES)
        u, q, k, v, ccol, crow = _mixer_in(x2, row(norm_mix_pre[l]), wag, wqkv, wf, bfp, tri, seq)
        att = _fox_attn(q, k, v, ccol, crow, batch, seq)
        x2 = _mixer_out(x2, u, att, conv_w[l], row(conv_b[l]), row(conv_ln_g[l]),
                        row(conv_ln_b[l]), w_out[l].astype(BF16), row(norm_mix_post[l]), seq)
        km, vm = _mem_kv(mem2, row(norm_memkv[l]), w_mk[l].astype(BF16), w_mv[l].astype(BF16), batch)
        x2 = _mem_attn(x2, row(norm_mem_pre[l]), w_mq[l].astype(BF16), km, vm,
                       w_mo[l].astype(BF16), row(norm_mem_post[l]), seq)
        x2 = _mlp(x2, row(norm_mlp_pre[l]), w_up[l].astype(BF16), w_down[l].astype(BF16),
                  row(norm_mlp_post[l]))
    return x2.reshape(batch, seq, d)
```

```python
import functools

import jax
import jax.numpy as jnp
from jax import lax
from jax.experimental import pallas as pl
from jax.experimental.pallas import tpu as pltpu

D_MODEL = 1024
CONV_CH = 512
CONV_WIDTH = 31
FOX_HEADS = 8
FOX_HEAD_DIM = 64
FOX_WIDTH = FOX_HEADS * FOX_HEAD_DIM
N_MEM = 256
MEM_HEADS = 4
MEM_HEAD_DIM = 128
MEM_INNER = MEM_HEADS * MEM_HEAD_DIM
D_FF = 4 * D_MODEL
EPS = 1e-6
NEG_INF = -1e30
LOG2E = 1.4426950408889634

LANES = 128
SUBLANES = 8
MIB = 1024 * 1024

TM_IN = 512
TM_OUT = 256
CONV_HALO = 32
CONV_ROWS = 32
TQ = 256
TK = 256
HEAD_GROUP = 4
GROUP_W = HEAD_GROUP * FOX_HEAD_DIM
TM_MEM = 512
TM_MLP = 512
FF_CHUNK = 1024

BF16 = jnp.bfloat16
F32 = jnp.float32


def _rms(x, gain):
    return x * lax.rsqrt(jnp.mean(x * x, axis=-1, keepdims=True) + EPS) * gain


def _sigmoid(x):
    return 1.0 / (1.0 + jnp.exp(-x))


def _log_sigmoid(x):
    return -(jnp.maximum(-x, 0.0) + jnp.log1p(jnp.exp(-jnp.abs(x))))


def _split3(x):
    hi = x.astype(BF16)
    r1 = x - hi.astype(F32)
    mid = r1.astype(BF16)
    lo = (r1 - mid.astype(F32)).astype(BF16)
    return hi, mid, lo


def _mixer_in_kernel(tiles_per_seq, x_ref, g_ref, wag_ref, wqkv_ref, wf_ref, bf_ref, tri_ref,
                     u_ref, q_ref, k_ref, v_ref, ccol_ref, crow_ref, carry_ref):
    i = pl.program_id(0)

    @pl.when(i % tiles_per_seq == 0)
    def _():
        carry_ref[...] = jnp.zeros_like(carry_ref)

    h = _rms(x_ref[...], g_ref[...]).astype(BF16)
    ag = jnp.dot(h, wag_ref[...], preferred_element_type=F32)
    u_ref[...] = ag[:, :CONV_CH] * _sigmoid(ag[:, CONV_CH:])
    qkv = jnp.dot(h, wqkv_ref[...], preferred_element_type=F32)
    q_ref[...] = (qkv[:, :FOX_WIDTH] * (FOX_HEAD_DIM ** -0.5 * LOG2E)).astype(BF16)
    k_ref[...] = qkv[:, FOX_WIDTH:2 * FOX_WIDTH].astype(BF16)
    v_ref[...] = qkv[:, 2 * FOX_WIDTH:].astype(BF16)
    f = jnp.dot(h, wf_ref[...], preferred_element_type=F32) + bf_ref[...]
    logf = _log_sigmoid(f)
    tri = tri_ref[...]
    hi, mid, lo = _split3(logf)
    cum = (jnp.dot(tri, hi, preferred_element_type=F32)
           + jnp.dot(tri, mid, preferred_element_type=F32)
           + jnp.dot(tri, lo, preferred_element_type=F32)) + carry_ref[0:1, :]
    carry_ref[...] = jnp.broadcast_to(cum[TM_IN - 1:TM_IN, :], carry_ref.shape)
    cum2 = cum * LOG2E
    ccol_ref[...] = cum2
    crow_ref[...] = cum2.T[:FOX_HEADS, :]


def _mixer_in(x2, gain, wag, wqkv, wf, bfp, tri, seq):
    t = x2.shape[0]
    const = lambda i: (0, 0)
    row = lambda i: (i, 0)
    return pl.pallas_call(
        functools.partial(_mixer_in_kernel, seq // TM_IN),
        grid=(t // TM_IN,),
        in_specs=[
            pl.BlockSpec((TM_IN, D_MODEL), row),
            pl.BlockSpec((1, D_MODEL), const),
            pl.BlockSpec((D_MODEL, 2 * CONV_CH), const),
            pl.BlockSpec((D_MODEL, 3 * FOX_WIDTH), const),
            pl.BlockSpec((D_MODEL, LANES), const),
            pl.BlockSpec((1, LANES), const),
            pl.BlockSpec((TM_IN, TM_IN), const),
        ],
        out_specs=[
            pl.BlockSpec((TM_IN, CONV_CH), row),
            pl.BlockSpec((TM_IN, FOX_WIDTH), row),
            pl.BlockSpec((TM_IN, FOX_WIDTH), row),
            pl.BlockSpec((TM_IN, FOX_WIDTH), row),
            pl.BlockSpec((TM_IN, LANES), row),
            pl.BlockSpec((FOX_HEADS, TM_IN), lambda i: (0, i)),
        ],
        out_shape=[
            jax.ShapeDtypeStruct((t, CONV_CH), F32),
            jax.ShapeDtypeStruct((t, FOX_WIDTH), BF16),
            jax.ShapeDtypeStruct((t, FOX_WIDTH), BF16),
            jax.ShapeDtypeStruct((t, FOX_WIDTH), BF16),
            jax.ShapeDtypeStruct((t, LANES), F32),
            jax.ShapeDtypeStruct((FOX_HEADS, t), F32),
        ],
        scratch_shapes=[pltpu.VMEM((SUBLANES, LANES), F32)],
        compiler_params=pltpu.CompilerParams(
            dimension_semantics=("arbitrary",), vmem_limit_bytes=48 * MIB),
        name="mixer_in",
    )(x2, gain, wag, wqkv, wf, bfp, tri)


def _fox_attn_kernel(q_ref, k_ref, v_ref, ccol_ref, crow_ref, hmask_ref, o_ref,
                     qcat_ref, s_ref, mx_ref, r_ref, l_ref, acc_ref):
    i = pl.program_id(1)
    n_groups = FOX_HEADS // HEAD_GROUP
    for g in range(n_groups):
        qg = q_ref[:, g * GROUP_W:(g + 1) * GROUP_W]
        for h in range(HEAD_GROUP):
            qcat_ref[g, h * TQ:(h + 1) * TQ, :] = qg * hmask_ref[h:h + 1, :]
    mx_ref[...] = jnp.full(mx_ref.shape, NEG_INF, F32)

    def scores(j, diagonal):
        ks = pl.multiple_of(j * TK, TK)
        for g in range(n_groups):
            kj = k_ref[pl.ds(ks, TK), g * GROUP_W:(g + 1) * GROUP_W]
            s = lax.dot_general(qcat_ref[g], kj, (((1,), (1,)), ((), ())),
                                preferred_element_type=F32)
            for h in range(HEAD_GROUP):
                hg = g * HEAD_GROUP + h
                t = s[h * TQ:(h + 1) * TQ, :] - crow_ref[hg:hg + 1, pl.ds(ks, TK)]
                if diagonal:
                    row_pos = lax.broadcasted_iota(jnp.int32, (TQ, TK), 0)
                    col_pos = lax.broadcasted_iota(jnp.int32, (TQ, TK), 1)
                    t = jnp.where(col_pos <= row_pos, t, NEG_INF)
                s_ref[hg, :, pl.ds(ks, TK)] = t
                m = mx_ref[hg]
                for c in range(TK // LANES):
                    m = jnp.maximum(m, t[:, c * LANES:(c + 1) * LANES])
                mx_ref[hg] = m

    def scores_body(j, carry):
        scores(j, False)
        return carry

    lax.fori_loop(0, i, scores_body, 0)
    scores(i, True)

    for hg in range(FOX_HEADS):
        cq = ccol_ref[:, hg:hg + 1]
        m = jnp.max(mx_ref[hg], axis=-1, keepdims=True) + cq
        r_ref[hg] = jnp.broadcast_to(cq - m, (TQ, LANES))
    l_ref[...] = jnp.zeros(l_ref.shape, F32)
    acc_ref[...] = jnp.zeros(acc_ref.shape, F32)

    def pv_body(j, carry):
        ks = pl.multiple_of(j * TK, TK)
        for hg in range(FOX_HEADS):
            pair = hg // 2
            r = r_ref[hg]
            lsum = l_ref[hg]
            ps = []
            for c in range(TK // LANES):
                p = jnp.exp2(s_ref[hg, :, pl.ds(ks + c * LANES, LANES)] + r)
                lsum = lsum + p
                ps.append(p.astype(BF16))
            l_ref[hg] = lsum
            acc_ref[hg] += jnp.dot(jnp.concatenate(ps, axis=1),
                                   v_ref[pl.ds(ks, TK), pair * LANES:(pair + 1) * LANES],
                                   preferred_element_type=F32)
        return carry

    lax.fori_loop(0, i + 1, pv_body, 0)

    lane = lax.broadcasted_iota(jnp.int32, (TQ, LANES), 1)
    for pair in range(FOX_HEADS // 2):
        even = acc_ref[2 * pair] / jnp.sum(l_ref[2 * pair], axis=-1, keepdims=True)
        odd = acc_ref[2 * pair + 1] / jnp.sum(l_ref[2 * pair + 1], axis=-1, keepdims=True)
        o_ref[:, pair * LANES:(pair + 1) * LANES] = jnp.where(lane < FOX_HEAD_DIM, even, odd).astype(BF16)


def _fox_attn(q, k, v, ccol, crow, batch, seq):
    t = q.shape[0]
    nq = seq // TQ
    hmask = (jnp.arange(GROUP_W)[None, :] // FOX_HEAD_DIM == jnp.arange(HEAD_GROUP)[:, None]).astype(BF16)
    qrow = lambda b, i: (b * nq + i, 0)
    whole_seq = lambda b, i: (b, 0)
    return pl.pallas_call(
        _fox_attn_kernel,
        grid=(batch, nq),
        in_specs=[
            pl.BlockSpec((TQ, FOX_WIDTH), qrow),
            pl.BlockSpec((seq, FOX_WIDTH), whole_seq, pipeline_mode=pl.Buffered(1)),
            pl.BlockSpec((seq, FOX_WIDTH), whole_seq, pipeline_mode=pl.Buffered(1)),
            pl.BlockSpec((TQ, LANES), qrow),
            pl.BlockSpec((FOX_HEADS, seq), lambda b, i: (0, b)),
            pl.BlockSpec((HEAD_GROUP, GROUP_W), lambda b, i: (0, 0)),
        ],
        out_specs=pl.BlockSpec((TQ, FOX_WIDTH), qrow),
        out_shape=jax.ShapeDtypeStruct((t, FOX_WIDTH), BF16),
        scratch_shapes=[
            pltpu.VMEM((FOX_HEADS // HEAD_GROUP, HEAD_GROUP * TQ, GROUP_W), BF16),
            pltpu.VMEM((FOX_HEADS, TQ, seq), F32),
            pltpu.VMEM((FOX_HEADS, TQ, LANES), F32),
            pltpu.VMEM((FOX_HEADS, TQ, LANES), F32),
            pltpu.VMEM((FOX_HEADS, TQ, LANES), F32),
            pltpu.VMEM((FOX_HEADS, TQ, LANES), F32),
        ],
        compiler_params=pltpu.CompilerParams(
            dimension_semantics=("arbitrary", "arbitrary"), vmem_limit_bytes=58 * MIB),
        name="fox_attn",
    )(q, k, v, ccol, crow, hmask)


def _mixer_out_kernel(tiles_per_seq, x_ref, u_ref, uprev_ref, att_ref, cw_ref, cb_ref, lng_ref,
                      lnb_ref, wo_ref, gpost_ref, o_ref, ubuf_ref, cat_ref):
    i = pl.program_id(0)
    first = (i % tiles_per_seq == 0)
    ubuf_ref[0:CONV_HALO, :] = jnp.where(first, 0.0, uprev_ref[...])
    ubuf_ref[CONV_HALO:, :] = u_ref[...]
    base = CONV_HALO - (CONV_WIDTH - 1)
    for c in range(TM_OUT // CONV_ROWS):
        r0 = c * CONV_ROWS
        acc = jnp.broadcast_to(cb_ref[...], (CONV_ROWS, CONV_CH))
        for kk in range(CONV_WIDTH):
            acc = acc + cw_ref[kk:kk + 1, :] * ubuf_ref[r0 + base + kk:r0 + base + kk + CONV_ROWS, :]
        mu = jnp.mean(acc, axis=-1, keepdims=True)
        xc = acc - mu
        y = xc * lax.rsqrt(jnp.mean(xc * xc, axis=-1, keepdims=True) + EPS)
        y = y * lng_ref[...] + lnb_ref[...]
        cat_ref[r0:r0 + CONV_ROWS, 0:CONV_CH] = (y * _sigmoid(y)).astype(BF16)
    cat_ref[:, CONV_CH:] = att_ref[...]
    y = jnp.dot(cat_ref[...], wo_ref[...], preferred_element_type=F32)
    o_ref[...] = x_ref[...] + _rms(y, gpost_ref[...])


def _mixer_out(x2, u, att, cw, cb, lng, lnb, wo, gpost, seq):
    t = x2.shape[0]
    const = lambda i: (0, 0)
    row = lambda i: (i, 0)
    halo_blocks = TM_OUT // CONV_HALO
    return pl.pallas_call(
        functools.partial(_mixer_out_kernel, seq // TM_OUT),
        grid=(t // TM_OUT,),
        in_specs=[
            pl.BlockSpec((TM_OUT, D_MODEL), row),
            pl.BlockSpec((TM_OUT, CONV_CH), row),
            pl.BlockSpec((CONV_HALO, CONV_CH), lambda i: (jnp.maximum(i * halo_blocks - 1, 0), 0)),
            pl.BlockSpec((TM_OUT, FOX_WIDTH), row),
            pl.BlockSpec((CONV_WIDTH, CONV_CH), const),
            pl.BlockSpec((1, CONV_CH), const),
            pl.BlockSpec((1, CONV_CH), const),
            pl.BlockSpec((1, CONV_CH), const),
            pl.BlockSpec((CONV_CH + FOX_WIDTH, D_MODEL), const),
            pl.BlockSpec((1, D_MODEL), const),
        ],
        out_specs=pl.BlockSpec((TM_OUT, D_MODEL), row),
        out_shape=jax.ShapeDtypeStruct((t, D_MODEL), F32),
        scratch_shapes=[
            pltpu.VMEM((CONV_HALO + TM_OUT, CONV_CH), F32),
            pltpu.VMEM((TM_OUT, CONV_CH + FOX_WIDTH), BF16),
        ],
        compiler_params=pltpu.CompilerParams(
            dimension_semantics=("arbitrary",), vmem_limit_bytes=32 * MIB),
        name="mixer_out",
    )(x2, u, u, att, cw, cb, lng, lnb, wo, gpost)


def _mem_kv_kernel(mem_ref, g_ref, wk_ref, wv_ref, k_ref, v_ref):
    mn = _rms(mem_ref[...], g_ref[...]).astype(BF16)
    k_ref[...] = jnp.dot(mn, wk_ref[...], preferred_element_type=F32).astype(BF16)
    v_ref[...] = jnp.dot(mn, wv_ref[...], preferred_element_type=F32).astype(BF16)


def _mem_kv(mem2, gain, wk, wv, batch):
    const = lambda b: (0, 0)
    row = lambda b: (b, 0)
    return pl.pallas_call(
        _mem_kv_kernel,
        grid=(batch,),
        in_specs=[
            pl.BlockSpec((N_MEM, D_MODEL), row),
            pl.BlockSpec((1, D_MODEL), const),
            pl.BlockSpec((D_MODEL, MEM_INNER), const),
            pl.BlockSpec((D_MODEL, MEM_INNER), const),
        ],
        out_specs=[pl.BlockSpec((N_MEM, MEM_INNER), row), pl.BlockSpec((N_MEM, MEM_INNER), row)],
        out_shape=[jax.ShapeDtypeStruct((batch * N_MEM, MEM_INNER), BF16)] * 2,
        compiler_params=pltpu.CompilerParams(dimension_semantics=("arbitrary",)),
        name="mem_kv",
    )(mem2, gain, wk, wv)


def _mem_attn_kernel(x_ref, gpre_ref, wq_ref, km_ref, vm_ref, wo_ref, gpost_ref, o_ref, cat_ref):
    x = x_ref[...]
    h = _rms(x, gpre_ref[...]).astype(BF16)
    q = jnp.dot(h, wq_ref[...], preferred_element_type=F32)
    for hd in range(MEM_HEADS):
        lanes = slice(hd * MEM_HEAD_DIM, (hd + 1) * MEM_HEAD_DIM)
        s = lax.dot_general(q[:, lanes].astype(BF16), km_ref[:, lanes], (((1,), (1,)), ((), ())),
                            preferred_element_type=F32) * (MEM_HEAD_DIM ** -0.5)
        p = jnp.exp(s - jnp.max(s, axis=-1, keepdims=True))
        l = jnp.sum(p, axis=-1, keepdims=True)
        o = jnp.dot(p.astype(BF16), vm_ref[:, lanes], preferred_element_type=F32)
        cat_ref[:, lanes] = (o / l).astype(BF16)
    y = jnp.dot(cat_ref[...], wo_ref[...], preferred_element_type=F32)
    o_ref[...] = x + _rms(y, gpost_ref[...])


def _mem_attn(x2, gpre, wq, km, vm, wo, gpost, seq):
    t = x2.shape[0]
    tiles_per_seq = seq // TM_MEM
    const = lambda i: (0, 0)
    row = lambda i: (i, 0)
    per_batch = lambda i: (i // tiles_per_seq, 0)
    return pl.pallas_call(
        _mem_attn_kernel,
        grid=(t // TM_MEM,),
        in_specs=[
            pl.BlockSpec((TM_MEM, D_MODEL), row),
            pl.BlockSpec((1, D_MODEL), const),
            pl.BlockSpec((D_MODEL, MEM_INNER), const),
            pl.BlockSpec((N_MEM, MEM_INNER), per_batch),
            pl.BlockSpec((N_MEM, MEM_INNER), per_batch),
            pl.BlockSpec((MEM_INNER, D_MODEL), const),
            pl.BlockSpec((1, D_MODEL), const),
        ],
        out_specs=pl.BlockSpec((TM_MEM, D_MODEL), row),
        out_shape=jax.ShapeDtypeStruct((t, D_MODEL), F32),
        scratch_shapes=[pltpu.VMEM((TM_MEM, MEM_INNER), BF16)],
        compiler_params=pltpu.CompilerParams(
            dimension_semantics=("arbitrary",), vmem_limit_bytes=32 * MIB),
        name="mem_attn",
    )(x2, gpre, wq, km, vm, wo, gpost)


def _mlp_kernel(x_ref, gpre_ref, wup_ref, wdn_ref, gpost_ref, o_ref):
    x = x_ref[...]
    h = _rms(x, gpre_ref[...]).astype(BF16)
    y = jnp.zeros((TM_MLP, D_MODEL), F32)
    for c in range(D_FF // FF_CHUNK):
        cols = slice(c * FF_CHUNK, (c + 1) * FF_CHUNK)
        a = jnp.maximum(jnp.dot(h, wup_ref[:, cols], preferred_element_type=F32), 0.0)
        y = y + jnp.dot((a * a).astype(BF16), wdn_ref[cols, :], preferred_element_type=F32)
    o_ref[...] = x + _rms(y, gpost_ref[...])


def _mlp(x2, gpre, wup, wdn, gpost):
    t = x2.shape[0]
    const = lambda i: (0, 0)
    row = lambda i: (i, 0)
    return pl.pallas_call(
        _mlp_kernel,
        grid=(t // TM_MLP,),
        in_specs=[
            pl.BlockSpec((TM_MLP, D_MODEL), row),
            pl.BlockSpec((1, D_MODEL), const),
            pl.BlockSpec((D_MODEL, D_FF), const),
            pl.BlockSpec((D_FF, D_MODEL), const),
            pl.BlockSpec((1, D_MODEL), const),
        ],
        out_specs=pl.BlockSpec((TM_MLP, D_MODEL), row),
        out_shape=jax.ShapeDtypeStruct((t, D_MODEL), F32),
        compiler_params=pltpu.CompilerParams(
            dimension_semantics=("arbitrary",), vmem_limit_bytes=56 * MIB),
        name="mlp",
    )(x2, gpre, wup, wdn, gpost)


def kernel(x, mem, norm_mix_pre, norm_mix_post, w_in, b_forget, conv_w, conv_b, conv_ln_g, conv_ln_b, w_out, norm_mem_pre, norm_mem_post, norm_memkv, w_mq, w_mk, w_mv, w_mo, norm_mlp_pre, norm_mlp_post, w_up, w_down):
    batch, seq, d = x.shape
    depth = w_in.shape[0]
    assert d == D_MODEL and seq % TM_IN == 0 and seq % TQ == 0 and mem.shape[1] == N_MEM
    x2 = x.reshape(batch * seq, d)
    mem2 = mem.reshape(batch * N_MEM, d)
    tri = (jnp.arange(TM_IN)[:, None] >= jnp.arange(TM_IN)[None, :]).astype(BF16)
    row = lambda a: a.reshape(1, -1)
    f_pad = LANES - FOX_HEADS
    for l in range(depth):
        wag = w_in[l, :, :2 * CONV_CH].astype(BF16)
        wqkv = w_in[l, :, 2 * CONV_CH:2 * CONV_CH + 3 * FOX_WIDTH].astype(BF16)
        wf = jnp.pad(w_in[l, :, 2 * CONV_CH + 3 * FOX_WIDTH:], ((0, 0), (0, f_pad))).astype(BF16)
        bfp = jnp.pad(b_forget[l], (0, f_pad)).reshape(1, LANES)
        u, q, k, v, ccol, crow = _mixer_in(x2, row(norm_mix_pre[l]), wag, wqkv, wf, bfp, tri, seq)
        att = _fox_attn(q, k, v, ccol, crow, batch, seq)
        x2 = _mixer_out(x2, u, att, conv_w[l], row(conv_b[l]), row(conv_ln_g[l]),
                        row(conv_ln_b[l]), w_out[l].astype(BF16), row(norm_mix_post[l]), seq)
        km, vm = _mem_kv(mem2, row(norm_memkv[l]), w_mk[l].astype(BF16), w_mv[l].astype(BF16), batch)
        x2 = _mem_attn(x2, row(norm_mem_pre[l]), w_mq[l].astype(BF16), km, vm,
                       w_mo[l].astype(BF16), row(norm_mem_post[l]), seq)
        x2 = _mlp(x2, row(norm_mlp_pre[l]), w_up[l].astype(BF16), w_down[l].astype(BF16),
                  row(norm_mlp_post[l]))
    return x2.reshape(batch, seq, d)
```

```python
import functools

import jax
import jax.numpy as jnp
from jax import lax
from jax.experimental import pallas as pl
from jax.experimental.pallas import tpu as pltpu

D_MODEL = 1024
CONV_CH = 512
CONV_WIDTH = 31
FOX_HEADS = 8
FOX_HEAD_DIM = 64
FOX_WIDTH = FOX_HEADS * FOX_HEAD_DIM
N_MEM = 256
MEM_HEADS = 4
MEM_HEAD_DIM = 128
MEM_INNER = MEM_HEADS * MEM_HEAD_DIM
D_FF = 4 * D_MODEL
EPS = 1e-6
NEG_INF = -1e30
LOG2E = 1.4426950408889634

LANES = 128
SUBLANES = 8
MIB = 1024 * 1024

TM_IN = 512
TM_OUT = 256
CONV_HALO = 32
CONV_ROWS = 32
TQ = 256
TK = 256
HEAD_GROUP = 4
GROUP_W = HEAD_GROUP * FOX_HEAD_DIM
V_SLABS_W = FOX_HEADS * LANES
KV_UNROLL = 4
TM_MEM = 512
TM_MLP = 512
FF_CHUNK = 1024

BF16 = jnp.bfloat16
F32 = jnp.float32


def _rms(x, gain):
    return x * lax.rsqrt(jnp.mean(x * x, axis=-1, keepdims=True) + EPS) * gain


def _sigmoid(x):
    return 1.0 / (1.0 + jnp.exp(-x))


def _log_sigmoid(x):
    return -(jnp.maximum(-x, 0.0) + jnp.log1p(jnp.exp(-jnp.abs(x))))


def _split3(x):
    hi = x.astype(BF16)
    r1 = x - hi.astype(F32)
    mid = r1.astype(BF16)
    lo = (r1 - mid.astype(F32)).astype(BF16)
    return hi, mid, lo


def _mixer_in_kernel(tiles_per_seq, x_ref, g_ref, wag_ref, wqkv_ref, wf_ref, bf_ref, tri_ref,
                     u_ref, q_ref, k_ref, v_ref, ccol_ref, crow_ref, carry_ref):
    i = pl.program_id(0)

    @pl.when(i % tiles_per_seq == 0)
    def _():
        carry_ref[...] = jnp.zeros_like(carry_ref)

    h = _rms(x_ref[...], g_ref[...]).astype(BF16)
    ag = jnp.dot(h, wag_ref[...], preferred_element_type=F32)
    u_ref[...] = ag[:, :CONV_CH] * _sigmoid(ag[:, CONV_CH:])
    qkv = jnp.dot(h, wqkv_ref[...], preferred_element_type=F32)
    q_ref[...] = (qkv[:, :FOX_WIDTH] * (FOX_HEAD_DIM ** -0.5 * LOG2E)).astype(BF16)
    k_ref[...] = qkv[:, FOX_WIDTH:2 * FOX_WIDTH].astype(BF16)
    lane = lax.broadcasted_iota(jnp.int32, (TM_IN, LANES), 1)
    for hd in range(FOX_HEADS):
        pair = qkv[:, 2 * FOX_WIDTH + (hd // 2) * LANES:2 * FOX_WIDTH + (hd // 2 + 1) * LANES]
        own = (lane < FOX_HEAD_DIM) if hd % 2 == 0 else (lane >= FOX_HEAD_DIM)
        v_ref[:, hd * LANES:(hd + 1) * LANES] = jnp.where(own, pair, 1.0).astype(BF16)
    f = jnp.dot(h, wf_ref[...], preferred_element_type=F32) + bf_ref[...]
    logf = _log_sigmoid(f)
    tri = tri_ref[...]
    hi, mid, lo = _split3(logf)
    cum = (jnp.dot(tri, hi, preferred_element_type=F32)
           + jnp.dot(tri, mid, preferred_element_type=F32)
           + jnp.dot(tri, lo, preferred_element_type=F32)) + carry_ref[0:1, :]
    carry_ref[...] = jnp.broadcast_to(cum[TM_IN - 1:TM_IN, :], carry_ref.shape)
    cum2 = cum * LOG2E
    ccol_ref[...] = cum2
    crow_ref[...] = cum2.T[:FOX_HEADS, :]


def _mixer_in(x2, gain, wag, wqkv, wf, bfp, tri, seq):
    t = x2.shape[0]
    const = lambda i: (0, 0)
    row = lambda i: (i, 0)
    return pl.pallas_call(
        functools.partial(_mixer_in_kernel, seq // TM_IN),
        grid=(t // TM_IN,),
        in_specs=[
            pl.BlockSpec((TM_IN, D_MODEL), row),
            pl.BlockSpec((1, D_MODEL), const),
            pl.BlockSpec((D_MODEL, 2 * CONV_CH), const),
            pl.BlockSpec((D_MODEL, 3 * FOX_WIDTH), const),
            pl.BlockSpec((D_MODEL, LANES), const),
            pl.BlockSpec((1, LANES), const),
            pl.BlockSpec((TM_IN, TM_IN), const),
        ],
        out_specs=[
            pl.BlockSpec((TM_IN, CONV_CH), row),
            pl.BlockSpec((TM_IN, FOX_WIDTH), row),
            pl.BlockSpec((TM_IN, FOX_WIDTH), row),
            pl.BlockSpec((TM_IN, V_SLABS_W), row),
            pl.BlockSpec((TM_IN, LANES), row),
            pl.BlockSpec((FOX_HEADS, TM_IN), lambda i: (0, i)),
        ],
        out_shape=[
            jax.ShapeDtypeStruct((t, CONV_CH), F32),
            jax.ShapeDtypeStruct((t, FOX_WIDTH), BF16),
            jax.ShapeDtypeStruct((t, FOX_WIDTH), BF16),
            jax.ShapeDtypeStruct((t, V_SLABS_W), BF16),
            jax.ShapeDtypeStruct((t, LANES), F32),
            jax.ShapeDtypeStruct((FOX_HEADS, t), F32),
        ],
        scratch_shapes=[pltpu.VMEM((SUBLANES, LANES), F32)],
        compiler_params=pltpu.CompilerParams(
            dimension_semantics=("arbitrary",), vmem_limit_bytes=48 * MIB),
        name="mixer_in",
    )(x2, gain, wag, wqkv, wf, bfp, tri)


def _fox_attn_kernel(q_ref, k_ref, v_ref, ccol_ref, crow_ref, hmask_ref, o_ref,
                     qcat_ref, s_ref, mx_ref, r_ref, acc_ref):
    i = pl.program_id(1)
    n_groups = FOX_HEADS // HEAD_GROUP
    for g in range(n_groups):
        qg = q_ref[:, g * GROUP_W:(g + 1) * GROUP_W]
        for h in range(HEAD_GROUP):
            qcat_ref[g, h * TQ:(h + 1) * TQ, :] = qg * hmask_ref[h:h + 1, :]
    mx_ref[...] = jnp.full(mx_ref.shape, NEG_INF, F32)

    def scores(j0, n_blocks):
        for u in range(n_blocks):
            ks = pl.multiple_of((j0 + u) * TK, TK)
            for g in range(n_groups):
                kj = k_ref[pl.ds(ks, TK), g * GROUP_W:(g + 1) * GROUP_W]
                s = lax.dot_general(qcat_ref[g], kj, (((1,), (1,)), ((), ())),
                                    preferred_element_type=F32)
                for h in range(HEAD_GROUP):
                    hg = g * HEAD_GROUP + h
                    t = s[h * TQ:(h + 1) * TQ, :] - crow_ref[hg:hg + 1, pl.ds(ks, TK)]
                    s_ref[hg, :, pl.ds(ks, TK)] = t
                    m = mx_ref[hg]
                    for c in range(TK // LANES):
                        m = jnp.maximum(m, t[:, c * LANES:(c + 1) * LANES])
                    mx_ref[hg] = m

    def loop_blocks(n, step):
        n_full = n // KV_UNROLL

        def unrolled(jj, carry):
            step(jj * KV_UNROLL, KV_UNROLL)
            return carry

        def single(j, carry):
            step(j, 1)
            return carry

        lax.fori_loop(0, n_full, unrolled, 0)
        lax.fori_loop(n_full * KV_UNROLL, n, single, 0)

    loop_blocks(i, scores)

    kd = pl.multiple_of(i * TK, TK)
    causal = (lax.broadcasted_iota(jnp.int32, (TQ, TK), 1)
              <= lax.broadcasted_iota(jnp.int32, (TQ, TK), 0))
    for g in range(n_groups):
        s = lax.dot_general(qcat_ref[g], k_ref[pl.ds(kd, TK), g * GROUP_W:(g + 1) * GROUP_W],
                            (((1,), (1,)), ((), ())), preferred_element_type=F32)
        for h in range(HEAD_GROUP):
            hg = g * HEAD_GROUP + h
            t = jnp.where(causal, s[h * TQ:(h + 1) * TQ, :] - crow_ref[hg:hg + 1, pl.ds(kd, TK)],
                          NEG_INF)
            m_lane = mx_ref[hg]
            for c in range(TK // LANES):
                m_lane = jnp.maximum(m_lane, t[:, c * LANES:(c + 1) * LANES])
            cq = ccol_ref[:, hg:hg + 1]
            m = jnp.max(m_lane, axis=-1, keepdims=True) + cq
            r = jnp.broadcast_to(cq - m, (TQ, LANES))
            r_ref[hg] = r
            ps = [jnp.exp2(t[:, c * LANES:(c + 1) * LANES] + r).astype(BF16)
                  for c in range(TK // LANES)]
            acc_ref[hg] = jnp.dot(jnp.concatenate(ps, axis=1),
                                  v_ref[pl.ds(kd, TK), hg * LANES:(hg + 1) * LANES],
                                  preferred_element_type=F32)

    def pv(j0, n_blocks):
        ks = pl.multiple_of(j0 * TK, TK)
        for hg in range(FOX_HEADS):
            r = r_ref[hg]
            ps = [jnp.exp2(s_ref[hg, :, pl.ds(ks + c * LANES, LANES)] + r).astype(BF16)
                  for c in range(n_blocks * TK // LANES)]
            acc_ref[hg] += jnp.dot(jnp.concatenate(ps, axis=1),
                                   v_ref[pl.ds(ks, n_blocks * TK), hg * LANES:(hg + 1) * LANES],
                                   preferred_element_type=F32)

    loop_blocks(i, pv)

    first_half = lax.broadcasted_iota(jnp.int32, (TQ, LANES), 1) < FOX_HEAD_DIM
    for pair in range(FOX_HEADS // 2):
        even = acc_ref[2 * pair]
        odd = acc_ref[2 * pair + 1]
        num = jnp.where(first_half, even, odd)
        den = pltpu.roll(jnp.where(first_half, odd, even), FOX_HEAD_DIM, axis=1)
        o_ref[:, pair * LANES:(pair + 1) * LANES] = (num / den).astype(BF16)


def _fox_attn(q, k, v, ccol, crow, batch, seq):
    t = q.shape[0]
    nq = seq // TQ
    hmask = (jnp.arange(GROUP_W)[None, :] // FOX_HEAD_DIM == jnp.arange(HEAD_GROUP)[:, None]).astype(BF16)
    qrow = lambda b, i: (b * nq + i, 0)
    whole_seq = lambda b, i: (b, 0)
    return pl.pallas_call(
        _fox_attn_kernel,
        grid=(batch, nq),
        in_specs=[
            pl.BlockSpec((TQ, FOX_WIDTH), qrow),
            pl.BlockSpec((seq, FOX_WIDTH), whole_seq, pipeline_mode=pl.Buffered(1)),
            pl.BlockSpec((seq, V_SLABS_W), whole_seq, pipeline_mode=pl.Buffered(1)),
            pl.BlockSpec((TQ, LANES), qrow),
            pl.BlockSpec((FOX_HEADS, seq), lambda b, i: (0, b)),
            pl.BlockSpec((HEAD_GROUP, GROUP_W), lambda b, i: (0, 0)),
        ],
        out_specs=pl.BlockSpec((TQ, FOX_WIDTH), qrow),
        out_shape=jax.ShapeDtypeStruct((t, FOX_WIDTH), BF16),
        scratch_shapes=[
            pltpu.VMEM((FOX_HEADS // HEAD_GROUP, HEAD_GROUP * TQ, GROUP_W), BF16),
            pltpu.VMEM((FOX_HEADS, TQ, seq), F32),
            pltpu.VMEM((FOX_HEADS, TQ, LANES), F32),
            pltpu.VMEM((FOX_HEADS, TQ, LANES), F32),
            pltpu.VMEM((FOX_HEADS, TQ, LANES), F32),
        ],
        compiler_params=pltpu.CompilerParams(
            dimension_semantics=("arbitrary", "arbitrary"), vmem_limit_bytes=58 * MIB),
        name="fox_attn",
    )(q, k, v, ccol, crow, hmask)


def _mixer_out_kernel(tiles_per_seq, x_ref, u_ref, uprev_ref, att_ref, cw_ref, cb_ref, lng_ref,
                      lnb_ref, wo_ref, gpost_ref, o_ref, ubuf_ref, cat_ref):
    i = pl.program_id(0)
    first = (i % tiles_per_seq == 0)
    ubuf_ref[0, 0:CONV_HALO, :] = jnp.where(first, 0.0, uprev_ref[...])
    ubuf_ref[0, CONV_HALO:, :] = u_ref[...]
    shifted_rows = CONV_HALO + TM_OUT - SUBLANES
    for s in range(1, SUBLANES):
        ubuf_ref[s, 0:shifted_rows, :] = ubuf_ref[0, s:s + shifted_rows, :]
    base = CONV_HALO - (CONV_WIDTH - 1)
    for c in range(TM_OUT // CONV_ROWS):
        r0 = c * CONV_ROWS
        acc = jnp.broadcast_to(cb_ref[...], (CONV_ROWS, CONV_CH))
        for kk in range(CONV_WIDTH):
            s = (base + kk) % SUBLANES
            a = r0 + base + kk - s
            acc = acc + cw_ref[kk:kk + 1, :] * ubuf_ref[s, a:a + CONV_ROWS, :]
        mu = jnp.mean(acc, axis=-1, keepdims=True)
        xc = acc - mu
        y = xc * lax.rsqrt(jnp.mean(xc * xc, axis=-1, keepdims=True) + EPS)
        y = y * lng_ref[...] + lnb_ref[...]
        cat_ref[r0:r0 + CONV_ROWS, 0:CONV_CH] = (y * _sigmoid(y)).astype(BF16)
    cat_ref[:, CONV_CH:] = att_ref[...]
    y = jnp.dot(cat_ref[...], wo_ref[...], preferred_element_type=F32)
    o_ref[...] = x_ref[...] + _rms(y, gpost_ref[...])


def _mixer_out(x2, u, att, cw, cb, lng, lnb, wo, gpost, seq):
    t = x2.shape[0]
    const = lambda i: (0, 0)
    row = lambda i: (i, 0)
    halo_blocks = TM_OUT // CONV_HALO
    return pl.pallas_call(
        functools.partial(_mixer_out_kernel, seq // TM_OUT),
        grid=(t // TM_OUT,),
        in_specs=[
            pl.BlockSpec((TM_OUT, D_MODEL), row),
            pl.BlockSpec((TM_OUT, CONV_CH), row),
            pl.BlockSpec((CONV_HALO, CONV_CH), lambda i: (jnp.maximum(i * halo_blocks - 1, 0), 0)),
            pl.BlockSpec((TM_OUT, FOX_WIDTH), row),
            pl.BlockSpec((CONV_WIDTH, CONV_CH), const),
            pl.BlockSpec((1, CONV_CH), const),
            pl.BlockSpec((1, CONV_CH), const),
            pl.BlockSpec((1, CONV_CH), const),
            pl.BlockSpec((CONV_CH + FOX_WIDTH, D_MODEL), const),
            pl.BlockSpec((1, D_MODEL), const),
        ],
        out_specs=pl.BlockSpec((TM_OUT, D_MODEL), row),
        out_shape=jax.ShapeDtypeStruct((t, D_MODEL), F32),
        scratch_shapes=[
            pltpu.VMEM((SUBLANES, CONV_HALO + TM_OUT, CONV_CH), F32),
            pltpu.VMEM((TM_OUT, CONV_CH + FOX_WIDTH), BF16),
        ],
        compiler_params=pltpu.CompilerParams(
            dimension_semantics=("arbitrary",), vmem_limit_bytes=32 * MIB),
        name="mixer_out",
    )(x2, u, u, att, cw, cb, lng, lnb, wo, gpost)


def _mem_kv_kernel(mem_ref, g_ref, wk_ref, wv_ref, k_ref, v_ref):
    mn = _rms(mem_ref[...], g_ref[...]).astype(BF16)
    k_ref[...] = jnp.dot(mn, wk_ref[...], preferred_element_type=F32).astype(BF16)
    v_ref[...] = jnp.dot(mn, wv_ref[...], preferred_element_type=F32).astype(BF16)


def _mem_kv(mem2, gain, wk, wv, batch):
    const = lambda b: (0, 0)
    row = lambda b: (b, 0)
    return pl.pallas_call(
        _mem_kv_kernel,
        grid=(batch,),
        in_specs=[
            pl.BlockSpec((N_MEM, D_MODEL), row),
            pl.BlockSpec((1, D_MODEL), const),
            pl.BlockSpec((D_MODEL, MEM_INNER), const),
            pl.BlockSpec((D_MODEL, MEM_INNER), const),
        ],
        out_specs=[pl.BlockSpec((N_MEM, MEM_INNER), row), pl.BlockSpec((N_MEM, MEM_INNER), row)],
        out_shape=[jax.ShapeDtypeStruct((batch * N_MEM, MEM_INNER), BF16)] * 2,
        compiler_params=pltpu.CompilerParams(dimension_semantics=("arbitrary",)),
        name="mem_kv",
    )(mem2, gain, wk, wv)


def _mem_attn_kernel(x_ref, gpre_ref, wq_ref, km_ref, vm_ref, wo_ref, gpost_ref, o_ref, cat_ref):
    x = x_ref[...]
    h = _rms(x, gpre_ref[...]).astype(BF16)
    q = jnp.dot(h, wq_ref[...], preferred_element_type=F32)
    for hd in range(MEM_HEADS):
        lanes = slice(hd * MEM_HEAD_DIM, (hd + 1) * MEM_HEAD_DIM)
        s = lax.dot_general(q[:, lanes].astype(BF16), km_ref[:, lanes], (((1,), (1,)), ((), ())),
                            preferred_element_type=F32) * (MEM_HEAD_DIM ** -0.5)
        p = jnp.exp(s - jnp.max(s, axis=-1, keepdims=True))
        l = jnp.sum(p, axis=-1, keepdims=True)
        o = jnp.dot(p.astype(BF16), vm_ref[:, lanes], preferred_element_type=F32)
        cat_ref[:, lanes] = (o / l).astype(BF16)
    y = jnp.dot(cat_ref[...], wo_ref[...], preferred_element_type=F32)
    o_ref[...] = x + _rms(y, gpost_ref[...])


def _mem_attn(x2, gpre, wq, km, vm, wo, gpost, seq):
    t = x2.shape[0]
    tiles_per_seq = seq // TM_MEM
    const = lambda i: (0, 0)
    row = lambda i: (i, 0)
    per_batch = lambda i: (i // tiles_per_seq, 0)
    return pl.pallas_call(
        _mem_attn_kernel,
        grid=(t // TM_MEM,),
        in_specs=[
            pl.BlockSpec((TM_MEM, D_MODEL), row),
            pl.BlockSpec((1, D_MODEL), const),
            pl.BlockSpec((D_MODEL, MEM_INNER), const),
            pl.BlockSpec((N_MEM, MEM_INNER), per_batch),
            pl.BlockSpec((N_MEM, MEM_INNER), per_batch),
            pl.BlockSpec((MEM_INNER, D_MODEL), const),
            pl.BlockSpec((1, D_MODEL), const),
        ],
        out_specs=pl.BlockSpec((TM_MEM, D_MODEL), row),
        out_shape=jax.ShapeDtypeStruct((t, D_MODEL), F32),
        scratch_shapes=[pltpu.VMEM((TM_MEM, MEM_INNER), BF16)],
        compiler_params=pltpu.CompilerParams(
            dimension_semantics=("arbitrary",), vmem_limit_bytes=32 * MIB),
        name="mem_attn",
    )(x2, gpre, wq, km, vm, wo, gpost)


def _mlp_kernel(x_ref, gpre_ref, wup_ref, wdn_ref, gpost_ref, o_ref):
    x = x_ref[...]
    h = _rms(x, gpre_ref[...]).astype(BF16)
    y = jnp.zeros((TM_MLP, D_MODEL), F32)
    for c in range(D_FF // FF_CHUNK):
        cols = slice(c * FF_CHUNK, (c + 1) * FF_CHUNK)
        a = jnp.maximum(jnp.dot(h, wup_ref[:, cols], preferred_element_type=F32), 0.0)
        y = y + jnp.dot((a * a).astype(BF16), wdn_ref[cols, :], preferred_element_type=F32)
    o_ref[...] = x + _rms(y, gpost_ref[...])


def _mlp(x2, gpre, wup, wdn, gpost):
    t = x2.shape[0]
    const = lambda i: (0, 0)
    row = lambda i: (i, 0)
    return pl.pallas_call(
        _mlp_kernel,
        grid=(t // TM_MLP,),
        in_specs=[
            pl.BlockSpec((TM_MLP, D_MODEL), row),
            pl.BlockSpec((1, D_MODEL), const),
            pl.BlockSpec((D_MODEL, D_FF), const),
            pl.BlockSpec((D_FF, D_MODEL), const),
            pl.BlockSpec((1, D_MODEL), const),
        ],
        out_specs=pl.BlockSpec((TM_MLP, D_MODEL), row),
        out_shape=jax.ShapeDtypeStruct((t, D_MODEL), F32),
        compiler_params=pltpu.CompilerParams(
            dimension_semantics=("arbitrary",), vmem_limit_bytes=56 * MIB),
        name="mlp",
    )(x2, gpre, wup, wdn, gpost)


def kernel(x, mem, norm_mix_pre, norm_mix_post, w_in, b_forget, conv_w, conv_b, conv_ln_g, conv_ln_b, w_out, norm_mem_pre, norm_mem_post, norm_memkv, w_mq, w_mk, w_mv, w_mo, norm_mlp_pre, norm_mlp_post, w_up, w_down):
    batch, seq, d = x.shape
    depth = w_in.shape[0]
    assert d == D_MODEL and seq % TM_IN == 0 and seq % TQ == 0 and mem.shape[1] == N_MEM
    x2 = x.reshape(batch * seq, d)
    mem2 = mem.reshape(batch * N_MEM, d)
    tri = (jnp.arange(TM_IN)[:, None] >= jnp.arange(TM_IN)[None, :]).astype(BF16)
    row = lambda a: a.reshape(1, -1)
    f_pad = LANES - FOX_HEADS
    for l in range(depth):
        wag = w_in[l, :, :2 * CONV_CH].astype(BF16)
        wqkv = w_in[l, :, 2 * CONV_CH:2 * CONV_CH + 3 * FOX_WIDTH].astype(BF16)
        wf = jnp.pad(w_in[l, :, 2 * CONV_CH + 3 * FOX_WIDTH:], ((0, 0), (0, f_pad))).astype(BF16)
        bfp = jnp.pad(b_forget[l], (0, f_pad)).reshape(1, LANES)
        u, q, k, v, ccol, crow = _mixer_in(x2, row(norm_mix_pre[l]), wag, wqkv, wf, bfp, tri, seq)
        att = _fox_attn(q, k, v, ccol, crow, batch, seq)
        x2 = _mixer_out(x2, u, att, conv_w[l], row(conv_b[l]), row(conv_ln_g[l]),
                        row(conv_ln_b[l]), w_out[l].astype(BF16), row(norm_mix_post[l]), seq)
        km, vm = _mem_kv(mem2, row(norm_memkv[l]), w_mk[l].astype(BF16), w_mv[l].astype(BF16), batch)
        x2 = _mem_attn(x2, row(norm_mem_pre[l]), w_mq[l].astype(BF16), km, vm,
                       w_mo[l].astype(BF16), row(norm_mem_post[l]), seq)
        x2 = _mlp(x2, row(norm_mlp_pre[l]), w_up[l].astype(BF16), w_down[l].astype(BF16),
                  row(norm_mlp_post[l]))
    return x2.reshape(batch, seq, d)
```

```python
import functools

import jax
import jax.numpy as jnp
from jax import lax
from jax.experimental import pallas as pl
from jax.experimental.pallas import tpu as pltpu

D_MODEL = 1024
CONV_CH = 512
CONV_WIDTH = 31
FOX_HEADS = 8
FOX_HEAD_DIM = 64
FOX_WIDTH = FOX_HEADS * FOX_HEAD_DIM
N_MEM = 256
MEM_HEADS = 4
MEM_HEAD_DIM = 128
MEM_INNER = MEM_HEADS * MEM_HEAD_DIM
D_FF = 4 * D_MODEL
EPS = 1e-6
NEG_INF = -1e30
LOG2E = 1.4426950408889634

LANES = 128
SUBLANES = 8
MIB = 1024 * 1024

TM_IN = 512
IN_SPLIT = 2
TM_OUT = 512
OUT_SPLIT = 2
CONV_HALO = 32
CONV_ROWS = 32
TQ = 256
TK = 256
HEAD_GROUP = 4
GROUP_W = HEAD_GROUP * FOX_HEAD_DIM
V_SLABS_W = FOX_HEADS * LANES
KV_UNROLL = 4
TM_MEM = 512
TM_MLP = 512
FF_CHUNK = 1024

BF16 = jnp.bfloat16
F32 = jnp.float32


def _rms(x, gain):
    return x * lax.rsqrt(jnp.mean(x * x, axis=-1, keepdims=True) + EPS) * gain


def _sigmoid(x):
    return 1.0 / (1.0 + jnp.exp(-x))


def _log_sigmoid(x):
    return -(jnp.maximum(-x, 0.0) + jnp.log1p(jnp.exp(-jnp.abs(x))))


def _split3(x):
    hi = x.astype(BF16)
    r1 = x - hi.astype(F32)
    mid = r1.astype(BF16)
    lo = (r1 - mid.astype(F32)).astype(BF16)
    return hi, mid, lo


def _mixer_in_kernel(tiles_per_seq, x_ref, g_ref, wag_ref, wqkv_ref, wf_ref, bf_ref, tri_ref,
                     u_ref, q_ref, k_ref, v_ref, ccol_ref, crow_ref, carry_ref):
    i = pl.program_id(0)

    @pl.when(i % tiles_per_seq == 0)
    def _():
        carry_ref[...] = jnp.zeros_like(carry_ref)

    lane = lax.broadcasted_iota(jnp.int32, (TM_IN, LANES), 1)
    carry = carry_ref[0:1, :]
    for part in range(IN_SPLIT):
        rows = slice(part * TM_IN, (part + 1) * TM_IN)
        h = _rms(x_ref[rows, :], g_ref[...]).astype(BF16)
        ag = jnp.dot(h, wag_ref[...], preferred_element_type=F32)
        u_ref[rows, :] = ag[:, :CONV_CH] * _sigmoid(ag[:, CONV_CH:])
        qkv = jnp.dot(h, wqkv_ref[...], preferred_element_type=F32)
        q_ref[rows, :] = (qkv[:, :FOX_WIDTH] * (FOX_HEAD_DIM ** -0.5 * LOG2E)).astype(BF16)
        k_ref[rows, :] = qkv[:, FOX_WIDTH:2 * FOX_WIDTH].astype(BF16)
        for hd in range(FOX_HEADS):
            pair = qkv[:, 2 * FOX_WIDTH + (hd // 2) * LANES:2 * FOX_WIDTH + (hd // 2 + 1) * LANES]
            own = (lane < FOX_HEAD_DIM) if hd % 2 == 0 else (lane >= FOX_HEAD_DIM)
            v_ref[rows, hd * LANES:(hd + 1) * LANES] = jnp.where(own, pair, 1.0).astype(BF16)
        f = jnp.dot(h, wf_ref[...], preferred_element_type=F32) + bf_ref[...]
        logf = _log_sigmoid(f)
        tri = tri_ref[...]
        hi, mid, lo = _split3(logf)
        cum = (jnp.dot(tri, hi, preferred_element_type=F32)
               + jnp.dot(tri, mid, preferred_element_type=F32)
               + jnp.dot(tri, lo, preferred_element_type=F32)) + carry
        carry = cum[TM_IN - 1:TM_IN, :]
        cum2 = cum * LOG2E
        ccol_ref[rows, :] = cum2
        crow_ref[:, rows] = cum2.T[:FOX_HEADS, :]
    carry_ref[...] = jnp.broadcast_to(carry, carry_ref.shape)


def _mixer_in(x2, gain, wag, wqkv, wf, bfp, tri, seq):
    t = x2.shape[0]
    const = lambda i: (0, 0)
    row = lambda i: (i, 0)
    tm = TM_IN * IN_SPLIT
    return pl.pallas_call(
        functools.partial(_mixer_in_kernel, seq // tm),
        grid=(t // tm,),
        in_specs=[
            pl.BlockSpec((tm, D_MODEL), row),
            pl.BlockSpec((1, D_MODEL), const),
            pl.BlockSpec((D_MODEL, 2 * CONV_CH), const),
            pl.BlockSpec((D_MODEL, 3 * FOX_WIDTH), const),
            pl.BlockSpec((D_MODEL, LANES), const),
            pl.BlockSpec((1, LANES), const),
            pl.BlockSpec((TM_IN, TM_IN), const),
        ],
        out_specs=[
            pl.BlockSpec((tm, CONV_CH), row),
            pl.BlockSpec((tm, FOX_WIDTH), row),
            pl.BlockSpec((tm, FOX_WIDTH), row),
            pl.BlockSpec((tm, V_SLABS_W), row),
            pl.BlockSpec((tm, LANES), row),
            pl.BlockSpec((FOX_HEADS, tm), lambda i: (0, i)),
        ],
        out_shape=[
            jax.ShapeDtypeStruct((t, CONV_CH), F32),
            jax.ShapeDtypeStruct((t, FOX_WIDTH), BF16),
            jax.ShapeDtypeStruct((t, FOX_WIDTH), BF16),
            jax.ShapeDtypeStruct((t, V_SLABS_W), BF16),
            jax.ShapeDtypeStruct((t, LANES), F32),
            jax.ShapeDtypeStruct((FOX_HEADS, t), F32),
        ],
        scratch_shapes=[pltpu.VMEM((SUBLANES, LANES), F32)],
        compiler_params=pltpu.CompilerParams(
            dimension_semantics=("arbitrary",), vmem_limit_bytes=48 * MIB),
        name="mixer_in",
    )(x2, gain, wag, wqkv, wf, bfp, tri)


def _fox_attn_kernel(q_ref, k_ref, v_ref, ccol_ref, crow_ref, hmask_ref, spread_ref, o_ref,
                     qcat_ref, s_ref, mx_ref, r_ref, acc_ref):
    i = pl.program_id(1)
    n_groups = FOX_HEADS // HEAD_GROUP
    for g in range(n_groups):
        qg = q_ref[:, g * GROUP_W:(g + 1) * GROUP_W]
        for h in range(HEAD_GROUP):
            qcat_ref[g, h * TQ:(h + 1) * TQ, :] = qg * hmask_ref[h:h + 1, :]
    mx_ref[...] = jnp.full(mx_ref.shape, NEG_INF, F32)

    def scores(j0, n_blocks):
        for u in range(n_blocks):
            ks = pl.multiple_of((j0 + u) * TK, TK)
            for g in range(n_groups):
                kj = k_ref[pl.ds(ks, TK), g * GROUP_W:(g + 1) * GROUP_W]
                s = lax.dot_general(qcat_ref[g], kj, (((1,), (1,)), ((), ())),
                                    preferred_element_type=F32)
                for h in range(HEAD_GROUP):
                    hg = g * HEAD_GROUP + h
                    t = s[h * TQ:(h + 1) * TQ, :] - crow_ref[hg:hg + 1, pl.ds(ks, TK)]
                    s_ref[hg, :, pl.ds(ks, TK)] = t
                    m = mx_ref[hg]
                    for c in range(TK // LANES):
                        m = jnp.maximum(m, t[:, c * LANES:(c + 1) * LANES])
                    mx_ref[hg] = m

    def loop_blocks(n, step):
        n_full = n // KV_UNROLL

        def unrolled(jj, carry):
            step(jj * KV_UNROLL, KV_UNROLL)
            return carry

        def single(j, carry):
            step(j, 1)
            return carry

        lax.fori_loop(0, n_full, unrolled, 0)
        lax.fori_loop(n_full * KV_UNROLL, n, single, 0)

    loop_blocks(i, scores)

    kd = pl.multiple_of(i * TK, TK)
    causal = (lax.broadcasted_iota(jnp.int32, (TQ, TK), 1)
              <= lax.broadcasted_iota(jnp.int32, (TQ, TK), 0))
    cq_rep = jnp.dot(ccol_ref[...].astype(BF16), spread_ref[...], preferred_element_type=F32)
    for g in range(n_groups):
        s = lax.dot_general(qcat_ref[g], k_ref[pl.ds(kd, TK), g * GROUP_W:(g + 1) * GROUP_W],
                            (((1,), (1,)), ((), ())), preferred_element_type=F32)
        for h in range(HEAD_GROUP):
            hg = g * HEAD_GROUP + h
            t = jnp.where(causal, s[h * TQ:(h + 1) * TQ, :] - crow_ref[hg:hg + 1, pl.ds(kd, TK)],
                          NEG_INF)
            m_lane = mx_ref[hg]
            for c in range(TK // LANES):
                m_lane = jnp.maximum(m_lane, t[:, c * LANES:(c + 1) * LANES])
            cq = cq_rep[:, hg * LANES:(hg + 1) * LANES]
            m = jnp.max(m_lane, axis=-1, keepdims=True) + cq
            r = cq - m
            r_ref[hg] = r
            ps = [jnp.exp2(t[:, c * LANES:(c + 1) * LANES] + r).astype(BF16)
                  for c in range(TK // LANES)]
            acc_ref[hg] = jnp.dot(jnp.concatenate(ps, axis=1),
                                  v_ref[pl.ds(kd, TK), hg * LANES:(hg + 1) * LANES],
                                  preferred_element_type=F32)

    def pv(j0, n_blocks):
        ks = pl.multiple_of(j0 * TK, TK)
        for hg in range(FOX_HEADS):
            r = r_ref[hg]
            ps = [jnp.exp2(s_ref[hg, :, pl.ds(ks + c * LANES, LANES)] + r).astype(BF16)
                  for c in range(n_blocks * TK // LANES)]
            acc_ref[hg] += jnp.dot(jnp.concatenate(ps, axis=1),
                                   v_ref[pl.ds(ks, n_blocks * TK), hg * LANES:(hg + 1) * LANES],
                                   preferred_element_type=F32)

    loop_blocks(i, pv)

    first_half = lax.broadcasted_iota(jnp.int32, (TQ, LANES), 1) < FOX_HEAD_DIM
    for pair in range(FOX_HEADS // 2):
        even = acc_ref[2 * pair]
        odd = acc_ref[2 * pair + 1]
        num = jnp.where(first_half, even, odd)
        den = pltpu.roll(jnp.where(first_half, odd, even), FOX_HEAD_DIM, axis=1)
        o_ref[:, pair * LANES:(pair + 1) * LANES] = (num / den).astype(BF16)


def _fox_attn(q, k, v, ccol, crow, batch, seq):
    t = q.shape[0]
    nq = seq // TQ
    hmask = (jnp.arange(GROUP_W)[None, :] // FOX_HEAD_DIM == jnp.arange(HEAD_GROUP)[:, None]).astype(BF16)
    spread = (jnp.arange(V_SLABS_W)[None, :] // LANES == jnp.arange(LANES)[:, None]).astype(BF16)
    qrow = lambda b, i: (b * nq + i, 0)
    whole_seq = lambda b, i: (b, 0)
    return pl.pallas_call(
        _fox_attn_kernel,
        grid=(batch, nq),
        in_specs=[
            pl.BlockSpec((TQ, FOX_WIDTH), qrow),
            pl.BlockSpec((seq, FOX_WIDTH), whole_seq, pipeline_mode=pl.Buffered(1)),
            pl.BlockSpec((seq, V_SLABS_W), whole_seq, pipeline_mode=pl.Buffered(1)),
            pl.BlockSpec((TQ, LANES), qrow),
            pl.BlockSpec((FOX_HEADS, seq), lambda b, i: (0, b)),
            pl.BlockSpec((HEAD_GROUP, GROUP_W), lambda b, i: (0, 0)),
            pl.BlockSpec((LANES, V_SLABS_W), lambda b, i: (0, 0)),
        ],
        out_specs=pl.BlockSpec((TQ, FOX_WIDTH), qrow),
        out_shape=jax.ShapeDtypeStruct((t, FOX_WIDTH), BF16),
        scratch_shapes=[
            pltpu.VMEM((FOX_HEADS // HEAD_GROUP, HEAD_GROUP * TQ, GROUP_W), BF16),
            pltpu.VMEM((FOX_HEADS, TQ, seq), F32),
            pltpu.VMEM((FOX_HEADS, TQ, LANES), F32),
            pltpu.VMEM((FOX_HEADS, TQ, LANES), F32),
            pltpu.VMEM((FOX_HEADS, TQ, LANES), F32),
        ],
        compiler_params=pltpu.CompilerParams(
            dimension_semantics=("arbitrary", "arbitrary"), vmem_limit_bytes=58 * MIB),
        name="fox_attn",
    )(q, k, v, ccol, crow, hmask, spread)


def _mixer_out_kernel(tiles_per_seq, x_ref, u_ref, uprev_ref, att_ref, cw_ref, cb_ref, lng_ref,
                      lnb_ref, wo_ref, gpost_ref, o_ref, ubuf_ref, cat_ref):
    i = pl.program_id(0)
    first = (i % tiles_per_seq == 0)
    ubuf_ref[0, 0:CONV_HALO, :] = jnp.where(first, 0.0, uprev_ref[...])
    ubuf_ref[0, CONV_HALO:, :] = u_ref[...]
    shifted_rows = CONV_HALO + TM_OUT - SUBLANES
    for s in range(1, SUBLANES):
        ubuf_ref[s, 0:shifted_rows, :] = ubuf_ref[0, s:s + shifted_rows, :]
    base = CONV_HALO - (CONV_WIDTH - 1)
    cat_ref[:, CONV_CH:] = att_ref[...]
    part_rows = TM_OUT // OUT_SPLIT
    for part in range(OUT_SPLIT):
        for c in range(part_rows // CONV_ROWS):
            r0 = part * part_rows + c * CONV_ROWS
            acc = jnp.broadcast_to(cb_ref[...], (CONV_ROWS, CONV_CH))
            for kk in range(CONV_WIDTH):
                s = (base + kk) % SUBLANES
                a = r0 + base + kk - s
                acc = acc + cw_ref[kk:kk + 1, :] * ubuf_ref[s, a:a + CONV_ROWS, :]
            mu = jnp.mean(acc, axis=-1, keepdims=True)
            xc = acc - mu
            y = xc * lax.rsqrt(jnp.mean(xc * xc, axis=-1, keepdims=True) + EPS)
            y = y * lng_ref[...] + lnb_ref[...]
            cat_ref[r0:r0 + CONV_ROWS, 0:CONV_CH] = (y * _sigmoid(y)).astype(BF16)
        rows = slice(part * part_rows, (part + 1) * part_rows)
        y = jnp.dot(cat_ref[rows, :], wo_ref[...], preferred_element_type=F32)
        o_ref[rows, :] = x_ref[rows, :] + _rms(y, gpost_ref[...])


def _mixer_out(x2, u, att, cw, cb, lng, lnb, wo, gpost, seq):
    t = x2.shape[0]
    const = lambda i: (0, 0)
    row = lambda i: (i, 0)
    halo_blocks = TM_OUT // CONV_HALO
    return pl.pallas_call(
        functools.partial(_mixer_out_kernel, seq // TM_OUT),
        grid=(t // TM_OUT,),
        in_specs=[
            pl.BlockSpec((TM_OUT, D_MODEL), row),
            pl.BlockSpec((TM_OUT, CONV_CH), row),
            pl.BlockSpec((CONV_HALO, CONV_CH), lambda i: (jnp.maximum(i * halo_blocks - 1, 0), 0)),
            pl.BlockSpec((TM_OUT, FOX_WIDTH), row),
            pl.BlockSpec((CONV_WIDTH, CONV_CH), const),
            pl.BlockSpec((1, CONV_CH), const),
            pl.BlockSpec((1, CONV_CH), const),
            pl.BlockSpec((1, CONV_CH), const),
            pl.BlockSpec((CONV_CH + FOX_WIDTH, D_MODEL), const),
            pl.BlockSpec((1, D_MODEL), const),
        ],
        out_specs=pl.BlockSpec((TM_OUT, D_MODEL), row),
        out_shape=jax.ShapeDtypeStruct((t, D_MODEL), F32),
        scratch_shapes=[
            pltpu.VMEM((SUBLANES, CONV_HALO + TM_OUT, CONV_CH), F32),
            pltpu.VMEM((TM_OUT, CONV_CH + FOX_WIDTH), BF16),
        ],
        compiler_params=pltpu.CompilerParams(
            dimension_semantics=("arbitrary",), vmem_limit_bytes=32 * MIB),
        name="mixer_out",
    )(x2, u, u, att, cw, cb, lng, lnb, wo, gpost)


def _mem_kv_kernel(mem_ref, g_ref, wk_ref, wv_ref, k_ref, v_ref):
    mn = _rms(mem_ref[...], g_ref[...]).astype(BF16)
    k_ref[...] = jnp.dot(mn, wk_ref[...], preferred_element_type=F32).astype(BF16)
    v_ref[...] = jnp.dot(mn, wv_ref[...], preferred_element_type=F32).astype(BF16)


def _mem_kv(mem2, gain, wk, wv, batch):
    const = lambda b: (0, 0)
    row = lambda b: (b, 0)
    return pl.pallas_call(
        _mem_kv_kernel,
        grid=(batch,),
        in_specs=[
            pl.BlockSpec((N_MEM, D_MODEL), row),
            pl.BlockSpec((1, D_MODEL), const),
            pl.BlockSpec((D_MODEL, MEM_INNER), const),
            pl.BlockSpec((D_MODEL, MEM_INNER), const),
        ],
        out_specs=[pl.BlockSpec((N_MEM, MEM_INNER), row), pl.BlockSpec((N_MEM, MEM_INNER), row)],
        out_shape=[jax.ShapeDtypeStruct((batch * N_MEM, MEM_INNER), BF16)] * 2,
        compiler_params=pltpu.CompilerParams(dimension_semantics=("arbitrary",)),
        name="mem_kv",
    )(mem2, gain, wk, wv)


def _mem_attn_kernel(x_ref, gpre_ref, wq_ref, km_ref, vm_ref, wo_ref, gpost_ref, o_ref, cat_ref):
    x = x_ref[...]
    h = _rms(x, gpre_ref[...]).astype(BF16)
    q = jnp.dot(h, wq_ref[...], preferred_element_type=F32)
    for hd in range(MEM_HEADS):
        lanes = slice(hd * MEM_HEAD_DIM, (hd + 1) * MEM_HEAD_DIM)
        s = lax.dot_general(q[:, lanes].astype(BF16), km_ref[:, lanes], (((1,), (1,)), ((), ())),
                            preferred_element_type=F32) * (MEM_HEAD_DIM ** -0.5)
        p = jnp.exp(s - jnp.max(s, axis=-1, keepdims=True))
        l = jnp.sum(p, axis=-1, keepdims=True)
        o = jnp.dot(p.astype(BF16), vm_ref[:, lanes], preferred_element_type=F32)
        cat_ref[:, lanes] = (o / l).astype(BF16)
    y = jnp.dot(cat_ref[...], wo_ref[...], preferred_element_type=F32)
    o_ref[...] = x + _rms(y, gpost_ref[...])


def _mem_attn(x2, gpre, wq, km, vm, wo, gpost, seq):
    t = x2.shape[0]
    tiles_per_seq = seq // TM_MEM
    const = lambda i: (0, 0)
    row = lambda i: (i, 0)
    per_batch = lambda i: (i // tiles_per_seq, 0)
    return pl.pallas_call(
        _mem_attn_kernel,
        grid=(t // TM_MEM,),
        in_specs=[
            pl.BlockSpec((TM_MEM, D_MODEL), row),
            pl.BlockSpec((1, D_MODEL), const),
            pl.BlockSpec((D_MODEL, MEM_INNER), const),
            pl.BlockSpec((N_MEM, MEM_INNER), per_batch),
            pl.BlockSpec((N_MEM, MEM_INNER), per_batch),
            pl.BlockSpec((MEM_INNER, D_MODEL), const),
            pl.BlockSpec((1, D_MODEL), const),
        ],
        out_specs=pl.BlockSpec((TM_MEM, D_MODEL), row),
        out_shape=jax.ShapeDtypeStruct((t, D_MODEL), F32),
        scratch_shapes=[pltpu.VMEM((TM_MEM, MEM_INNER), BF16)],
        compiler_params=pltpu.CompilerParams(
            dimension_semantics=("arbitrary",), vmem_limit_bytes=32 * MIB),
        name="mem_attn",
    )(x2, gpre, wq, km, vm, wo, gpost)


def _mlp_kernel(x_ref, gpre_ref, wup_ref, wdn_ref, gpost_ref, o_ref):
    x = x_ref[...]
    h = _rms(x, gpre_ref[...]).astype(BF16)
    y = jnp.zeros((TM_MLP, D_MODEL), F32)
    for c in range(D_FF // FF_CHUNK):
        cols = slice(c * FF_CHUNK, (c + 1) * FF_CHUNK)
        a = jnp.maximum(jnp.dot(h, wup_ref[:, cols], preferred_element_type=F32), 0.0)
        y = y + jnp.dot((a * a).astype(BF16), wdn_ref[cols, :], preferred_element_type=F32)
    o_ref[...] = x + _rms(y, gpost_ref[...])


def _mlp(x2, gpre, wup, wdn, gpost):
    t = x2.shape[0]
    const = lambda i: (0, 0)
    row = lambda i: (i, 0)
    return pl.pallas_call(
        _mlp_kernel,
        grid=(t // TM_MLP,),
        in_specs=[
            pl.BlockSpec((TM_MLP, D_MODEL), row),
            pl.BlockSpec((1, D_MODEL), const),
            pl.BlockSpec((D_MODEL, D_FF), const),
            pl.BlockSpec((D_FF, D_MODEL), const),
            pl.BlockSpec((1, D_MODEL), const),
        ],
        out_specs=pl.BlockSpec((TM_MLP, D_MODEL), row),
        out_shape=jax.ShapeDtypeStruct((t, D_MODEL), F32),
        compiler_params=pltpu.CompilerParams(
            dimension_semantics=("arbitrary",), vmem_limit_bytes=56 * MIB),
        name="mlp",
    )(x2, gpre, wup, wdn, gpost)


def kernel(x, mem, norm_mix_pre, norm_mix_post, w_in, b_forget, conv_w, conv_b, conv_ln_g, conv_ln_b, w_out, norm_mem_pre, norm_mem_post, norm_memkv, w_mq, w_mk, w_mv, w_mo, norm_mlp_pre, norm_mlp_post, w_up, w_down):
    batch, seq, d = x.shape
    depth = w_in.shape[0]
    assert d == D_MODEL and mem.shape[1] == N_MEM
    assert all(seq % tile == 0 for tile in (TM_IN * IN_SPLIT, TM_OUT, TQ, TM_MEM))
    x2 = x.reshape(batch * seq, d)
    mem2 = mem.reshape(batch * N_MEM, d)
    tri = (jnp.arange(TM_IN)[:, None] >= jnp.arange(TM_IN)[None, :]).astype(BF16)
    row = lambda a: a.reshape(1, -1)
    f_pad = LANES - FOX_HEADS
    for l in range(depth):
        wag = w_in[l, :, :2 * CONV_CH].astype(BF16)
        wqkv = w_in[l, :, 2 * CONV_CH:2 * CONV_CH + 3 * FOX_WIDTH].astype(BF16)
        wf = jnp.pad(w_in[l, :, 2 * CONV_CH + 3 * FOX_WIDTH:], ((0, 0), (0, f_pad))).astype(BF16)
        bfp = jnp.pad(b_forget[l], (0, f_pad)).reshape(1, LANES)
        u, q, k, v, ccol, crow = _mixer_in(x2, row(norm_mix_pre[l]), wag, wqkv, wf, bfp, tri, seq)
        att = _fox_attn(q, k, v, ccol, crow, batch, seq)
        x2 = _mixer_out(x2, u, att, conv_w[l], row(conv_b[l]), row(conv_ln_g[l]),
                        row(conv_ln_b[l]), w_out[l].astype(BF16), row(norm_mix_post[l]), seq)
        km, vm = _mem_kv(mem2, row(norm_memkv[l]), w_mk[l].astype(BF16), w_mv[l].astype(BF16), batch)
        x2 = _mem_attn(x2, row(norm_mem_pre[l]), w_mq[l].astype(BF16), km, vm,
                       w_mo[l].astype(BF16), row(norm_mem_post[l]), seq)
        x2 = _mlp(x2, row(norm_mlp_pre[l]), w_up[l].astype(BF16), w_down[l].astype(BF16),
                  row(norm_mlp_post[l]))
    return x2.reshape(batch, seq, d)
```

```python
import functools

import jax
import jax.numpy as jnp
from jax import lax
from jax.experimental import pallas as pl
from jax.experimental.pallas import tpu as pltpu

D_MODEL = 1024
CONV_CH = 512
CONV_WIDTH = 31
FOX_HEADS = 8
FOX_HEAD_DIM = 64
FOX_WIDTH = FOX_HEADS * FOX_HEAD_DIM
N_MEM = 256
MEM_HEADS = 4
MEM_HEAD_DIM = 128
MEM_INNER = MEM_HEADS * MEM_HEAD_DIM
D_FF = 4 * D_MODEL
EPS = 1e-6
NEG_INF = -1e30
LOG2E = 1.4426950408889634

LANES = 128
SUBLANES = 8
MIB = 1024 * 1024

TM_IN = 512
IN_SPLIT = 2
TM_OUT = 512
OUT_SPLIT = 2
CONV_HALO = 32
CONV_ROWS = 32
TQ = 256
TK = 256
HEAD_GROUP = 4
GROUP_W = HEAD_GROUP * FOX_HEAD_DIM
V_SLABS_W = FOX_HEADS * LANES
KV_UNROLLS = (8, 4, 2, 1)
TM_MEM = 1024
MEM_SPLIT = 2
TM_MLP = 1024
MLP_SPLIT = 2
FF_CHUNK = 1024

BF16 = jnp.bfloat16
F32 = jnp.float32


def _rms(x, gain):
    return x * lax.rsqrt(jnp.mean(x * x, axis=-1, keepdims=True) + EPS) * gain


def _sigmoid(x):
    return 1.0 / (1.0 + jnp.exp(-x))


def _log_sigmoid(x):
    return -(jnp.maximum(-x, 0.0) + jnp.log1p(jnp.exp(-jnp.abs(x))))


def _split3(x):
    hi = x.astype(BF16)
    r1 = x - hi.astype(F32)
    mid = r1.astype(BF16)
    lo = (r1 - mid.astype(F32)).astype(BF16)
    return hi, mid, lo


def _mixer_in_kernel(tiles_per_seq, x_ref, g_ref, wag_ref, wqkv_ref, wf_ref, bf_ref, tri_ref,
                     u_ref, q_ref, k_ref, v_ref, ccol_ref, crow_ref, carry_ref):
    i = pl.program_id(0)

    @pl.when(i % tiles_per_seq == 0)
    def _():
        carry_ref[...] = jnp.zeros_like(carry_ref)

    lane = lax.broadcasted_iota(jnp.int32, (TM_IN, LANES), 1)
    carry = carry_ref[0:1, :]
    for part in range(IN_SPLIT):
        rows = slice(part * TM_IN, (part + 1) * TM_IN)
        h = _rms(x_ref[rows, :], g_ref[...]).astype(BF16)
        ag = jnp.dot(h, wag_ref[...], preferred_element_type=F32)
        u_ref[rows, :] = ag[:, :CONV_CH] * _sigmoid(ag[:, CONV_CH:])
        qkv = jnp.dot(h, wqkv_ref[...], preferred_element_type=F32)
        q_ref[rows, :] = (qkv[:, :FOX_WIDTH] * (FOX_HEAD_DIM ** -0.5 * LOG2E)).astype(BF16)
        k_ref[rows, :] = qkv[:, FOX_WIDTH:2 * FOX_WIDTH].astype(BF16)
        for hd in range(FOX_HEADS):
            pair = qkv[:, 2 * FOX_WIDTH + (hd // 2) * LANES:2 * FOX_WIDTH + (hd // 2 + 1) * LANES]
            own = (lane < FOX_HEAD_DIM) if hd % 2 == 0 else (lane >= FOX_HEAD_DIM)
            v_ref[rows, hd * LANES:(hd + 1) * LANES] = jnp.where(own, pair, 1.0).astype(BF16)
        f = jnp.dot(h, wf_ref[...], preferred_element_type=F32) + bf_ref[...]
        logf = _log_sigmoid(f)
        tri = tri_ref[...]
        hi, mid, lo = _split3(logf)
        cum = (jnp.dot(tri, hi, preferred_element_type=F32)
               + jnp.dot(tri, mid, preferred_element_type=F32)
               + jnp.dot(tri, lo, preferred_element_type=F32)) + carry
        carry = cum[TM_IN - 1:TM_IN, :]
        cum2 = cum * LOG2E
        ccol_ref[rows, :] = cum2
        crow_ref[:, rows] = cum2.T[:FOX_HEADS, :]
    carry_ref[...] = jnp.broadcast_to(carry, carry_ref.shape)


def _mixer_in(x2, gain, wag, wqkv, wf, bfp, tri, seq):
    t = x2.shape[0]
    const = lambda i: (0, 0)
    row = lambda i: (i, 0)
    tm = TM_IN * IN_SPLIT
    return pl.pallas_call(
        functools.partial(_mixer_in_kernel, seq // tm),
        grid=(t // tm,),
        in_specs=[
            pl.BlockSpec((tm, D_MODEL), row),
            pl.BlockSpec((1, D_MODEL), const),
            pl.BlockSpec((D_MODEL, 2 * CONV_CH), const),
            pl.BlockSpec((D_MODEL, 3 * FOX_WIDTH), const),
            pl.BlockSpec((D_MODEL, LANES), const),
            pl.BlockSpec((1, LANES), const),
            pl.BlockSpec((TM_IN, TM_IN), const),
        ],
        out_specs=[
            pl.BlockSpec((tm, CONV_CH), row),
            pl.BlockSpec((tm, FOX_WIDTH), row),
            pl.BlockSpec((tm, FOX_WIDTH), row),
            pl.BlockSpec((tm, V_SLABS_W), row),
            pl.BlockSpec((tm, LANES), row),
            pl.BlockSpec((FOX_HEADS, tm), lambda i: (0, i)),
        ],
        out_shape=[
            jax.ShapeDtypeStruct((t, CONV_CH), F32),
            jax.ShapeDtypeStruct((t, FOX_WIDTH), BF16),
            jax.ShapeDtypeStruct((t, FOX_WIDTH), BF16),
            jax.ShapeDtypeStruct((t, V_SLABS_W), BF16),
            jax.ShapeDtypeStruct((t, LANES), F32),
            jax.ShapeDtypeStruct((FOX_HEADS, t), F32),
        ],
        scratch_shapes=[pltpu.VMEM((SUBLANES, LANES), F32)],
        compiler_params=pltpu.CompilerParams(
            dimension_semantics=("arbitrary",), vmem_limit_bytes=48 * MIB),
        name="mixer_in",
    )(x2, gain, wag, wqkv, wf, bfp, tri)


def _fox_attn_kernel(q_ref, k_ref, v_ref, ccol_ref, crow_ref, hmask_ref, spread_ref, o_ref,
                     qcat_ref, s_ref, mx_ref, r_ref, acc_ref):
    i = pl.program_id(1)
    n_groups = FOX_HEADS // HEAD_GROUP
    for g in range(n_groups):
        qg = q_ref[:, g * GROUP_W:(g + 1) * GROUP_W]
        for h in range(HEAD_GROUP):
            qcat_ref[g, h * TQ:(h + 1) * TQ, :] = qg * hmask_ref[h:h + 1, :]
    mx_ref[...] = jnp.full(mx_ref.shape, NEG_INF, F32)

    def scores(j0, n_blocks):
        for u in range(n_blocks):
            ks = pl.multiple_of((j0 + u) * TK, TK)
            for g in range(n_groups):
                kj = k_ref[pl.ds(ks, TK), g * GROUP_W:(g + 1) * GROUP_W]
                s = lax.dot_general(qcat_ref[g], kj, (((1,), (1,)), ((), ())),
                                    preferred_element_type=F32)
                for h in range(HEAD_GROUP):
                    hg = g * HEAD_GROUP + h
                    t = s[h * TQ:(h + 1) * TQ, :] - crow_ref[hg:hg + 1, pl.ds(ks, TK)]
                    s_ref[hg, :, pl.ds(ks, TK)] = t
                    m = mx_ref[hg]
                    for c in range(TK // LANES):
                        m = jnp.maximum(m, t[:, c * LANES:(c + 1) * LANES])
                    mx_ref[hg] = m

    def loop_blocks(n, step):
        start = 0
        for width in KV_UNROLLS:
            trips = (n - start) // width

            def body(jj, carry, width=width, start=start):
                step(start + jj * width, width)
                return carry

            lax.fori_loop(0, trips, body, 0)
            start = start + trips * width

    loop_blocks(i, scores)

    kd = pl.multiple_of(i * TK, TK)
    causal = (lax.broadcasted_iota(jnp.int32, (TQ, TK), 1)
              <= lax.broadcasted_iota(jnp.int32, (TQ, TK), 0))
    cq_rep = jnp.dot(ccol_ref[...].astype(BF16), spread_ref[...], preferred_element_type=F32)
    for g in range(n_groups):
        s = lax.dot_general(qcat_ref[g], k_ref[pl.ds(kd, TK), g * GROUP_W:(g + 1) * GROUP_W],
                            (((1,), (1,)), ((), ())), preferred_element_type=F32)
        for h in range(HEAD_GROUP):
            hg = g * HEAD_GROUP + h
            t = jnp.where(causal, s[h * TQ:(h + 1) * TQ, :] - crow_ref[hg:hg + 1, pl.ds(kd, TK)],
                          NEG_INF)
            m_lane = mx_ref[hg]
            for c in range(TK // LANES):
                m_lane = jnp.maximum(m_lane, t[:, c * LANES:(c + 1) * LANES])
            cq = cq_rep[:, hg * LANES:(hg + 1) * LANES]
            m = jnp.max(m_lane, axis=-1, keepdims=True) + cq
            r = cq - m
            r_ref[hg] = r
            ps = [jnp.exp2(t[:, c * LANES:(c + 1) * LANES] + r).astype(BF16)
                  for c in range(TK // LANES)]
            acc_ref[hg] = jnp.dot(jnp.concatenate(ps, axis=1),
                                  v_ref[pl.ds(kd, TK), hg * LANES:(hg + 1) * LANES],
                                  preferred_element_type=F32)

    def pv(j0, n_blocks):
        ks = pl.multiple_of(j0 * TK, TK)
        for hg in range(FOX_HEADS):
            r = r_ref[hg]
            acc = acc_ref[hg]
            for b in range(n_blocks):
                ps = [jnp.exp2(s_ref[hg, :, pl.ds(ks + b * TK + c * LANES, LANES)] + r).astype(BF16)
                      for c in range(TK // LANES)]
                acc = acc + jnp.dot(jnp.concatenate(ps, axis=1),
                                    v_ref[pl.ds(ks + b * TK, TK), hg * LANES:(hg + 1) * LANES],
                                    preferred_element_type=F32)
            acc_ref[hg] = acc

    loop_blocks(i, pv)

    first_half = lax.broadcasted_iota(jnp.int32, (TQ, LANES), 1) < FOX_HEAD_DIM
    for pair in range(FOX_HEADS // 2):
        even = acc_ref[2 * pair]
        odd = acc_ref[2 * pair + 1]
        num = jnp.where(first_half, even, odd)
        den = pltpu.roll(jnp.where(first_half, odd, even), FOX_HEAD_DIM, axis=1)
        o_ref[:, pair * LANES:(pair + 1) * LANES] = (num / den).astype(BF16)


def _fox_attn(q, k, v, ccol, crow, batch, seq):
    t = q.shape[0]
    nq = seq // TQ
    hmask = (jnp.arange(GROUP_W)[None, :] // FOX_HEAD_DIM == jnp.arange(HEAD_GROUP)[:, None]).astype(BF16)
    spread = (jnp.arange(V_SLABS_W)[None, :] // LANES == jnp.arange(LANES)[:, None]).astype(BF16)
    qrow = lambda b, i: (b * nq + i, 0)
    whole_seq = lambda b, i: (b, 0)
    return pl.pallas_call(
        _fox_attn_kernel,
        grid=(batch, nq),
        in_specs=[
            pl.BlockSpec((TQ, FOX_WIDTH), qrow),
            pl.BlockSpec((seq, FOX_WIDTH), whole_seq, pipeline_mode=pl.Buffered(1)),
            pl.BlockSpec((seq, V_SLABS_W), whole_seq, pipeline_mode=pl.Buffered(1)),
            pl.BlockSpec((TQ, LANES), qrow),
            pl.BlockSpec((FOX_HEADS, seq), lambda b, i: (0, b)),
            pl.BlockSpec((HEAD_GROUP, GROUP_W), lambda b, i: (0, 0)),
            pl.BlockSpec((LANES, V_SLABS_W), lambda b, i: (0, 0)),
        ],
        out_specs=pl.BlockSpec((TQ, FOX_WIDTH), qrow),
        out_shape=jax.ShapeDtypeStruct((t, FOX_WIDTH), BF16),
        scratch_shapes=[
            pltpu.VMEM((FOX_HEADS // HEAD_GROUP, HEAD_GROUP * TQ, GROUP_W), BF16),
            pltpu.VMEM((FOX_HEADS, TQ, seq), F32),
            pltpu.VMEM((FOX_HEADS, TQ, LANES), F32),
            pltpu.VMEM((FOX_HEADS, TQ, LANES), F32),
            pltpu.VMEM((FOX_HEADS, TQ, LANES), F32),
        ],
        compiler_params=pltpu.CompilerParams(
            dimension_semantics=("arbitrary", "arbitrary"), vmem_limit_bytes=58 * MIB),
        name="fox_attn",
    )(q, k, v, ccol, crow, hmask, spread)


def _mixer_out_kernel(tiles_per_seq, x_ref, u_ref, uprev_ref, att_ref, cw_ref, cb_ref, lng_ref,
                      lnb_ref, wo_ref, gpost_ref, o_ref, ubuf_ref, cat_ref):
    i = pl.program_id(0)
    first = (i % tiles_per_seq == 0)
    ubuf_ref[0, 0:CONV_HALO, :] = jnp.where(first, 0.0, uprev_ref[...])
    ubuf_ref[0, CONV_HALO:, :] = u_ref[...]
    shifted_rows = CONV_HALO + TM_OUT - SUBLANES
    for s in range(1, SUBLANES):
        ubuf_ref[s, 0:shifted_rows, :] = ubuf_ref[0, s:s + shifted_rows, :]
    base = CONV_HALO - (CONV_WIDTH - 1)
    cat_ref[:, CONV_CH:] = att_ref[...]
    part_rows = TM_OUT // OUT_SPLIT
    for part in range(OUT_SPLIT):
        for c in range(part_rows // CONV_ROWS):
            r0 = part * part_rows + c * CONV_ROWS
            acc = jnp.broadcast_to(cb_ref[...], (CONV_ROWS, CONV_CH))
            for kk in range(CONV_WIDTH):
                s = (base + kk) % SUBLANES
                a = r0 + base + kk - s
                acc = acc + cw_ref[kk:kk + 1, :] * ubuf_ref[s, a:a + CONV_ROWS, :]
            mu = jnp.mean(acc, axis=-1, keepdims=True)
            xc = acc - mu
            y = xc * lax.rsqrt(jnp.mean(xc * xc, axis=-1, keepdims=True) + EPS)
            y = y * lng_ref[...] + lnb_ref[...]
            cat_ref[r0:r0 + CONV_ROWS, 0:CONV_CH] = (y * _sigmoid(y)).astype(BF16)
        rows = slice(part * part_rows, (part + 1) * part_rows)
        y = jnp.dot(cat_ref[rows, :], wo_ref[...], preferred_element_type=F32)
        o_ref[rows, :] = x_ref[rows, :] + _rms(y, gpost_ref[...])


def _mixer_out(x2, u, att, cw, cb, lng, lnb, wo, gpost, seq):
    t = x2.shape[0]
    const = lambda i: (0, 0)
    row = lambda i: (i, 0)
    halo_blocks = TM_OUT // CONV_HALO
    return pl.pallas_call(
        functools.partial(_mixer_out_kernel, seq // TM_OUT),
        grid=(t // TM_OUT,),
        in_specs=[
            pl.BlockSpec((TM_OUT, D_MODEL), row),
            pl.BlockSpec((TM_OUT, CONV_CH), row),
            pl.BlockSpec((CONV_HALO, CONV_CH), lambda i: (jnp.maximum(i * halo_blocks - 1, 0), 0)),
            pl.BlockSpec((TM_OUT, FOX_WIDTH), row),
            pl.BlockSpec((CONV_WIDTH, CONV_CH), const),
            pl.BlockSpec((1, CONV_CH), const),
            pl.BlockSpec((1, CONV_CH), const),
            pl.BlockSpec((1, CONV_CH), const),
            pl.BlockSpec((CONV_CH + FOX_WIDTH, D_MODEL), const),
            pl.BlockSpec((1, D_MODEL), const),
        ],
        out_specs=pl.BlockSpec((TM_OUT, D_MODEL), row),
        out_shape=jax.ShapeDtypeStruct((t, D_MODEL), F32),
        scratch_shapes=[
            pltpu.VMEM((SUBLANES, CONV_HALO + TM_OUT, CONV_CH), F32),
            pltpu.VMEM((TM_OUT, CONV_CH + FOX_WIDTH), BF16),
        ],
        compiler_params=pltpu.CompilerParams(
            dimension_semantics=("arbitrary",), vmem_limit_bytes=32 * MIB),
        name="mixer_out",
    )(x2, u, u, att, cw, cb, lng, lnb, wo, gpost)


def _mem_kv_kernel(mem_ref, g_ref, wk_ref, wv_ref, k_ref, v_ref):
    mn = _rms(mem_ref[...], g_ref[...]).astype(BF16)
    k_ref[...] = jnp.dot(mn, wk_ref[...], preferred_element_type=F32).astype(BF16)
    v_ref[...] = jnp.dot(mn, wv_ref[...], preferred_element_type=F32).astype(BF16)


def _mem_kv(mem2, gain, wk, wv, batch):
    const = lambda b: (0, 0)
    row = lambda b: (b, 0)
    return pl.pallas_call(
        _mem_kv_kernel,
        grid=(batch,),
        in_specs=[
            pl.BlockSpec((N_MEM, D_MODEL), row),
            pl.BlockSpec((1, D_MODEL), const),
            pl.BlockSpec((D_MODEL, MEM_INNER), const),
            pl.BlockSpec((D_MODEL, MEM_INNER), const),
        ],
        out_specs=[pl.BlockSpec((N_MEM, MEM_INNER), row), pl.BlockSpec((N_MEM, MEM_INNER), row)],
        out_shape=[jax.ShapeDtypeStruct((batch * N_MEM, MEM_INNER), BF16)] * 2,
        compiler_params=pltpu.CompilerParams(dimension_semantics=("arbitrary",)),
        name="mem_kv",
    )(mem2, gain, wk, wv)


def _mem_attn_kernel(x_ref, gpre_ref, wq_ref, km_ref, vm_ref, wo_ref, gpost_ref, o_ref, cat_ref):
    part_rows = TM_MEM // MEM_SPLIT
    for part in range(MEM_SPLIT):
        rows = slice(part * part_rows, (part + 1) * part_rows)
        x = x_ref[rows, :]
        h = _rms(x, gpre_ref[...]).astype(BF16)
        q = jnp.dot(h, wq_ref[...], preferred_element_type=F32)
        for hd in range(MEM_HEADS):
            lanes = slice(hd * MEM_HEAD_DIM, (hd + 1) * MEM_HEAD_DIM)
            s = lax.dot_general(q[:, lanes].astype(BF16), km_ref[:, lanes], (((1,), (1,)), ((), ())),
                                preferred_element_type=F32) * (MEM_HEAD_DIM ** -0.5)
            p = jnp.exp(s - jnp.max(s, axis=-1, keepdims=True))
            l = jnp.sum(p, axis=-1, keepdims=True)
            o = jnp.dot(p.astype(BF16), vm_ref[:, lanes], preferred_element_type=F32)
            cat_ref[rows, lanes] = (o / l).astype(BF16)
        y = jnp.dot(cat_ref[rows, :], wo_ref[...], preferred_element_type=F32)
        o_ref[rows, :] = x + _rms(y, gpost_ref[...])


def _mem_attn(x2, gpre, wq, km, vm, wo, gpost, seq):
    t = x2.shape[0]
    tiles_per_seq = seq // TM_MEM
    const = lambda i: (0, 0)
    row = lambda i: (i, 0)
    per_batch = lambda i: (i // tiles_per_seq, 0)
    return pl.pallas_call(
        _mem_attn_kernel,
        grid=(t // TM_MEM,),
        in_specs=[
            pl.BlockSpec((TM_MEM, D_MODEL), row),
            pl.BlockSpec((1, D_MODEL), const),
            pl.BlockSpec((D_MODEL, MEM_INNER), const),
            pl.BlockSpec((N_MEM, MEM_INNER), per_batch),
            pl.BlockSpec((N_MEM, MEM_INNER), per_batch),
            pl.BlockSpec((MEM_INNER, D_MODEL), const),
            pl.BlockSpec((1, D_MODEL), const),
        ],
        out_specs=pl.BlockSpec((TM_MEM, D_MODEL), row),
        out_shape=jax.ShapeDtypeStruct((t, D_MODEL), F32),
        scratch_shapes=[pltpu.VMEM((TM_MEM, MEM_INNER), BF16)],
        compiler_params=pltpu.CompilerParams(
            dimension_semantics=("arbitrary",), vmem_limit_bytes=32 * MIB),
        name="mem_attn",
    )(x2, gpre, wq, km, vm, wo, gpost)


def _mlp_kernel(x_ref, gpre_ref, wup_ref, wdn_ref, gpost_ref, o_ref):
    part_rows = TM_MLP // MLP_SPLIT
    for part in range(MLP_SPLIT):
        rows = slice(part * part_rows, (part + 1) * part_rows)
        x = x_ref[rows, :]
        h = _rms(x, gpre_ref[...]).astype(BF16)
        y = jnp.zeros((part_rows, D_MODEL), F32)
        for c in range(D_FF // FF_CHUNK):
            cols = slice(c * FF_CHUNK, (c + 1) * FF_CHUNK)
            a = jnp.maximum(jnp.dot(h, wup_ref[:, cols], preferred_element_type=F32), 0.0)
            y = y + jnp.dot((a * a).astype(BF16), wdn_ref[cols, :], preferred_element_type=F32)
        o_ref[rows, :] = x + _rms(y, gpost_ref[...])


def _mlp(x2, gpre, wup, wdn, gpost):
    t = x2.shape[0]
    const = lambda i: (0, 0)
    row = lambda i: (i, 0)
    return pl.pallas_call(
        _mlp_kernel,
        grid=(t // TM_MLP,),
        in_specs=[
            pl.BlockSpec((TM_MLP, D_MODEL), row),
            pl.BlockSpec((1, D_MODEL), const),
            pl.BlockSpec((D_MODEL, D_FF), const, pipeline_mode=pl.Buffered(1)),
            pl.BlockSpec((D_FF, D_MODEL), const, pipeline_mode=pl.Buffered(1)),
            pl.BlockSpec((1, D_MODEL), const),
        ],
        out_specs=pl.BlockSpec((TM_MLP, D_MODEL), row),
        out_shape=jax.ShapeDtypeStruct((t, D_MODEL), F32),
        compiler_params=pltpu.CompilerParams(
            dimension_semantics=("arbitrary",), vmem_limit_bytes=56 * MIB),
        name="mlp",
    )(x2, gpre, wup, wdn, gpost)


def kernel(x, mem, norm_mix_pre, norm_mix_post, w_in, b_forget, conv_w, conv_b, conv_ln_g, conv_ln_b, w_out, norm_mem_pre, norm_mem_post, norm_memkv, w_mq, w_mk, w_mv, w_mo, norm_mlp_pre, norm_mlp_post, w_up, w_down):
    batch, seq, d = x.shape
    depth = w_in.shape[0]
    assert d == D_MODEL and mem.shape[1] == N_MEM
    assert all(seq % tile == 0 for tile in (TM_IN * IN_SPLIT, TM_OUT, TQ, TM_MEM))
    x2 = x.reshape(batch * seq, d)
    mem2 = mem.reshape(batch * N_MEM, d)
    tri = (jnp.arange(TM_IN)[:, None] >= jnp.arange(TM_IN)[None, :]).astype(BF16)
    row = lambda a: a.reshape(1, -1)
    f_pad = LANES - FOX_HEADS
    for l in range(depth):
        wag = w_in[l, :, :2 * CONV_CH].astype(BF16)
        wqkv = w_in[l, :, 2 * CONV_CH:2 * CONV_CH + 3 * FOX_WIDTH].astype(BF16)
        wf = jnp.pad(w_in[l, :, 2 * CONV_CH + 3 * FOX_WIDTH:], ((0, 0), (0, f_pad))).astype(BF16)
        bfp = jnp.pad(b_forget[l], (0, f_pad)).reshape(1, LANES)
        u, q, k, v, ccol, crow = _mixer_in(x2, row(norm_mix_pre[l]), wag, wqkv, wf, bfp, tri, seq)
        att = _fox_attn(q, k, v, ccol, crow, batch, seq)
        x2 = _mixer_out(x2, u, att, conv_w[l], row(conv_b[l]), row(conv_ln_g[l]),
                        row(conv_ln_b[l]), w_out[l].astype(BF16), row(norm_mix_post[l]), seq)
        km, vm = _mem_kv(mem2, row(norm_memkv[l]), w_mk[l].astype(BF16), w_mv[l].astype(BF16), batch)
        x2 = _mem_attn(x2, row(norm_mem_pre[l]), w_mq[l].astype(BF16), km, vm,
                       w_mo[l].astype(BF16), row(norm_mem_post[l]), seq)
        x2 = _mlp(x2, row(norm_mlp_pre[l]), w_up[l].astype(BF16), w_down[l].astype(BF16),
                  row(norm_mlp_post[l]))
    return x2.reshape(batch, seq, d)
```

```python
import functools

import jax
import jax.numpy as jnp
from jax import lax
from jax.experimental import pallas as pl
from jax.experimental.pallas import tpu as pltpu

D_MODEL = 1024
CONV_CH = 512
CONV_WIDTH = 31
FOX_HEADS = 8
FOX_HEAD_DIM = 64
FOX_WIDTH = FOX_HEADS * FOX_HEAD_DIM
N_MEM = 256
MEM_HEADS = 4
MEM_HEAD_DIM = 128
MEM_INNER = MEM_HEADS * MEM_HEAD_DIM
D_FF = 4 * D_MODEL
EPS = 1e-6
NEG_INF = -1e30
LOG2E = 1.4426950408889634

LANES = 128
SUBLANES = 8
MIB = 1024 * 1024

TM_IN = 512
IN_SPLIT = 2
TM_OUT = 1024
OUT_SPLIT = 4
CONV_HALO = 32
CONV_ROWS = 32
TQ = 256
TK = 256
HEAD_GROUP = 4
GROUP_W = HEAD_GROUP * FOX_HEAD_DIM
V_SLABS_W = FOX_HEADS * LANES
KV_UNROLLS = (8, 4, 2, 1)
TM_MEM = 1024
MEM_SPLIT = 2
TM_MLP = 1024
MLP_SPLIT = 2
FF_CHUNK = 1024

BF16 = jnp.bfloat16
F32 = jnp.float32


def _rms(x, gain):
    return x * lax.rsqrt(jnp.mean(x * x, axis=-1, keepdims=True) + EPS) * gain


def _sigmoid(x):
    return 1.0 / (1.0 + jnp.exp(-x))


def _log_sigmoid(x):
    return -(jnp.maximum(-x, 0.0) + jnp.log1p(jnp.exp(-jnp.abs(x))))


def _split3(x):
    hi = x.astype(BF16)
    r1 = x - hi.astype(F32)
    mid = r1.astype(BF16)
    lo = (r1 - mid.astype(F32)).astype(BF16)
    return hi, mid, lo


def _mixer_in_kernel(tiles_per_seq, x_ref, g_ref, wag_ref, wqkv_ref, bf_ref, tri_ref,
                     u_ref, q_ref, k_ref, v_ref, ccol_ref, crow_ref, carry_ref):
    i = pl.program_id(0)

    @pl.when(i % tiles_per_seq == 0)
    def _():
        carry_ref[...] = jnp.zeros_like(carry_ref)

    lane = lax.broadcasted_iota(jnp.int32, (TM_IN, LANES), 1)
    carry = carry_ref[0:1, :]
    for part in range(IN_SPLIT):
        rows = slice(part * TM_IN, (part + 1) * TM_IN)
        h = _rms(x_ref[rows, :], g_ref[...]).astype(BF16)
        ag = jnp.dot(h, wag_ref[...], preferred_element_type=F32)
        u_ref[rows, :] = ag[:, :CONV_CH] * _sigmoid(ag[:, CONV_CH:])
        qkv = jnp.dot(h, wqkv_ref[...], preferred_element_type=F32)
        q_ref[rows, :] = (qkv[:, :FOX_WIDTH] * (FOX_HEAD_DIM ** -0.5 * LOG2E)).astype(BF16)
        k_ref[rows, :] = qkv[:, FOX_WIDTH:2 * FOX_WIDTH].astype(BF16)
        for hd in range(FOX_HEADS):
            pair = qkv[:, 2 * FOX_WIDTH + (hd // 2) * LANES:2 * FOX_WIDTH + (hd // 2 + 1) * LANES]
            own = (lane < FOX_HEAD_DIM) if hd % 2 == 0 else (lane >= FOX_HEAD_DIM)
            v_ref[rows, hd * LANES:(hd + 1) * LANES] = jnp.where(own, pair, 1.0).astype(BF16)
        f = qkv[:, 3 * FOX_WIDTH:] + bf_ref[...]
        logf = _log_sigmoid(f)
        tri = tri_ref[...]
        parts = jnp.dot(tri, jnp.concatenate(_split3(logf), axis=1), preferred_element_type=F32)
        cum = (parts[:, :LANES] + parts[:, LANES:2 * LANES] + parts[:, 2 * LANES:]) + carry
        carry = cum[TM_IN - 1:TM_IN, :]
        cum2 = cum * LOG2E
        ccol_ref[rows, :] = cum2
        crow_ref[:, rows] = cum2.T[:FOX_HEADS, :]
    carry_ref[...] = jnp.broadcast_to(carry, carry_ref.shape)


def _mixer_in(x2, gain, wag, wqkv, bfp, tri, seq):
    t = x2.shape[0]
    const = lambda i: (0, 0)
    row = lambda i: (i, 0)
    tm = TM_IN * IN_SPLIT
    return pl.pallas_call(
        functools.partial(_mixer_in_kernel, seq // tm),
        grid=(t // tm,),
        in_specs=[
            pl.BlockSpec((tm, D_MODEL), row),
            pl.BlockSpec((1, D_MODEL), const),
            pl.BlockSpec((D_MODEL, 2 * CONV_CH), const),
            pl.BlockSpec((D_MODEL, 3 * FOX_WIDTH + LANES), const),
            pl.BlockSpec((1, LANES), const),
            pl.BlockSpec((TM_IN, TM_IN), const),
        ],
        out_specs=[
            pl.BlockSpec((tm, CONV_CH), row),
            pl.BlockSpec((tm, FOX_WIDTH), row),
            pl.BlockSpec((tm, FOX_WIDTH), row),
            pl.BlockSpec((tm, V_SLABS_W), row),
            pl.BlockSpec((tm, LANES), row),
            pl.BlockSpec((FOX_HEADS, tm), lambda i: (0, i)),
        ],
        out_shape=[
            jax.ShapeDtypeStruct((t, CONV_CH), F32),
            jax.ShapeDtypeStruct((t, FOX_WIDTH), BF16),
            jax.ShapeDtypeStruct((t, FOX_WIDTH), BF16),
            jax.ShapeDtypeStruct((t, V_SLABS_W), BF16),
            jax.ShapeDtypeStruct((t, LANES), F32),
            jax.ShapeDtypeStruct((FOX_HEADS, t), F32),
        ],
        scratch_shapes=[pltpu.VMEM((SUBLANES, LANES), F32)],
        compiler_params=pltpu.CompilerParams(
            dimension_semantics=("arbitrary",), vmem_limit_bytes=48 * MIB),
        name="mixer_in",
    )(x2, gain, wag, wqkv, bfp, tri)


def _fox_attn_kernel(q_ref, k_ref, v_ref, ccol_ref, crow_ref, hmask_ref, spread_ref, o_ref,
                     qcat_ref, s_ref, mx_ref, r_ref, acc_ref):
    i = pl.program_id(1)
    n_groups = FOX_HEADS // HEAD_GROUP
    for g in range(n_groups):
        qg = q_ref[:, g * GROUP_W:(g + 1) * GROUP_W]
        for h in range(HEAD_GROUP):
            qcat_ref[g, h * TQ:(h + 1) * TQ, :] = qg * hmask_ref[h:h + 1, :]
    mx_ref[...] = jnp.full(mx_ref.shape, NEG_INF, F32)

    def scores(j0, n_blocks):
        for u in range(n_blocks):
            ks = pl.multiple_of((j0 + u) * TK, TK)
            for g in range(n_groups):
                kj = k_ref[pl.ds(ks, TK), g * GROUP_W:(g + 1) * GROUP_W]
                s = lax.dot_general(qcat_ref[g], kj, (((1,), (1,)), ((), ())),
                                    preferred_element_type=F32)
                for h in range(HEAD_GROUP):
                    hg = g * HEAD_GROUP + h
                    t = s[h * TQ:(h + 1) * TQ, :] - crow_ref[hg:hg + 1, pl.ds(ks, TK)]
                    s_ref[hg, :, pl.ds(ks, TK)] = t
                    m = mx_ref[hg]
                    for c in range(TK // LANES):
                        m = jnp.maximum(m, t[:, c * LANES:(c + 1) * LANES])
                    mx_ref[hg] = m

    def loop_blocks(n, step):
        start = 0
        for width in KV_UNROLLS:
            trips = (n - start) // width

            def body(jj, carry, width=width, start=start):
                step(start + jj * width, width)
                return carry

            lax.fori_loop(0, trips, body, 0)
            start = start + trips * width

    loop_blocks(i, scores)

    kd = pl.multiple_of(i * TK, TK)
    causal = (lax.broadcasted_iota(jnp.int32, (TQ, TK), 1)
              <= lax.broadcasted_iota(jnp.int32, (TQ, TK), 0))
    cq_rep = jnp.dot(ccol_ref[...].astype(BF16), spread_ref[...], preferred_element_type=F32)
    for g in range(n_groups):
        s = lax.dot_general(qcat_ref[g], k_ref[pl.ds(kd, TK), g * GROUP_W:(g + 1) * GROUP_W],
                            (((1,), (1,)), ((), ())), preferred_element_type=F32)
        for h in range(HEAD_GROUP):
            hg = g * HEAD_GROUP + h
            t = jnp.where(causal, s[h * TQ:(h + 1) * TQ, :] - crow_ref[hg:hg + 1, pl.ds(kd, TK)],
                          NEG_INF)
            m_lane = mx_ref[hg]
            for c in range(TK // LANES):
                m_lane = jnp.maximum(m_lane, t[:, c * LANES:(c + 1) * LANES])
            cq = cq_rep[:, hg * LANES:(hg + 1) * LANES]
            m = jnp.max(m_lane, axis=-1, keepdims=True) + cq
            r = cq - m
            r_ref[hg] = r
            ps = [jnp.exp2(t[:, c * LANES:(c + 1) * LANES] + r).astype(BF16)
                  for c in range(TK // LANES)]
            acc_ref[hg] = jnp.dot(jnp.concatenate(ps, axis=1),
                                  v_ref[pl.ds(kd, TK), hg * LANES:(hg + 1) * LANES],
                                  preferred_element_type=F32)

    def pv(j0, n_blocks):
        ks = pl.multiple_of(j0 * TK, TK)
        for hg in range(FOX_HEADS):
            r = r_ref[hg]
            acc = acc_ref[hg]
            for b in range(n_blocks):
                ps = [jnp.exp2(s_ref[hg, :, pl.ds(ks + b * TK + c * LANES, LANES)] + r).astype(BF16)
                      for c in range(TK // LANES)]
                acc = acc + jnp.dot(jnp.concatenate(ps, axis=1),
                                    v_ref[pl.ds(ks + b * TK, TK), hg * LANES:(hg + 1) * LANES],
                                    preferred_element_type=F32)
            acc_ref[hg] = acc

    loop_blocks(i, pv)

    first_half = lax.broadcasted_iota(jnp.int32, (TQ, LANES), 1) < FOX_HEAD_DIM
    for pair in range(FOX_HEADS // 2):
        even = acc_ref[2 * pair]
        odd = acc_ref[2 * pair + 1]
        num = jnp.where(first_half, even, odd)
        den = pltpu.roll(jnp.where(first_half, odd, even), FOX_HEAD_DIM, axis=1)
        o_ref[:, pair * LANES:(pair + 1) * LANES] = (num / den).astype(BF16)


def _fox_attn(q, k, v, ccol, crow, batch, seq):
    t = q.shape[0]
    nq = seq // TQ
    hmask = (jnp.arange(GROUP_W)[None, :] // FOX_HEAD_DIM == jnp.arange(HEAD_GROUP)[:, None]).astype(BF16)
    spread = (jnp.arange(V_SLABS_W)[None, :] // LANES == jnp.arange(LANES)[:, None]).astype(BF16)
    qrow = lambda b, i: (b * nq + i, 0)
    whole_seq = lambda b, i: (b, 0)
    return pl.pallas_call(
        _fox_attn_kernel,
        grid=(batch, nq),
        in_specs=[
            pl.BlockSpec((TQ, FOX_WIDTH), qrow),
            pl.BlockSpec((seq, FOX_WIDTH), whole_seq, pipeline_mode=pl.Buffered(1)),
            pl.BlockSpec((seq, V_SLABS_W), whole_seq, pipeline_mode=pl.Buffered(1)),
            pl.BlockSpec((TQ, LANES), qrow),
            pl.BlockSpec((FOX_HEADS, seq), lambda b, i: (0, b)),
            pl.BlockSpec((HEAD_GROUP, GROUP_W), lambda b, i: (0, 0)),
            pl.BlockSpec((LANES, V_SLABS_W), lambda b, i: (0, 0)),
        ],
        out_specs=pl.BlockSpec((TQ, FOX_WIDTH), qrow),
        out_shape=jax.ShapeDtypeStruct((t, FOX_WIDTH), BF16),
        scratch_shapes=[
            pltpu.VMEM((FOX_HEADS // HEAD_GROUP, HEAD_GROUP * TQ, GROUP_W), BF16),
            pltpu.VMEM((FOX_HEADS, TQ, seq), F32),
            pltpu.VMEM((FOX_HEADS, TQ, LANES), F32),
            pltpu.VMEM((FOX_HEADS, TQ, LANES), F32),
            pltpu.VMEM((FOX_HEADS, TQ, LANES), F32),
        ],
        compiler_params=pltpu.CompilerParams(
            dimension_semantics=("arbitrary", "arbitrary"), vmem_limit_bytes=58 * MIB),
        name="fox_attn",
    )(q, k, v, ccol, crow, hmask, spread)


def _mixer_out_kernel(tiles_per_seq, x_ref, u_ref, uprev_ref, att_ref, cw_ref, cb_ref, lng_ref,
                      lnb_ref, wo_ref, gpost_ref, o_ref, ubuf_ref, cat_ref):
    i = pl.program_id(0)
    first = (i % tiles_per_seq == 0)
    ubuf_ref[0, 0:CONV_HALO, :] = jnp.where(first, 0.0, uprev_ref[...])
    ubuf_ref[0, CONV_HALO:, :] = u_ref[...]
    shifted_rows = CONV_HALO + TM_OUT - SUBLANES
    for s in range(1, SUBLANES):
        ubuf_ref[s, 0:shifted_rows, :] = ubuf_ref[0, s:s + shifted_rows, :]
    base = CONV_HALO - (CONV_WIDTH - 1)
    cat_ref[:, CONV_CH:] = att_ref[...]
    part_rows = TM_OUT // OUT_SPLIT
    for part in range(OUT_SPLIT):
        for c in range(part_rows // CONV_ROWS):
            r0 = part * part_rows + c * CONV_ROWS
            accs = [jnp.broadcast_to(cb_ref[...], (SUBLANES, CONV_CH))] * (CONV_ROWS // SUBLANES)
            for kk in range(CONV_WIDTH):
                s = (base + kk) % SUBLANES
                a = r0 + base + kk - s
                w8 = cw_ref[kk]
                accs = [acc + w8 * ubuf_ref[s, a + g * SUBLANES:a + (g + 1) * SUBLANES, :]
                        for g, acc in enumerate(accs)]
            acc = jnp.concatenate(accs, axis=0)
            mu = jnp.mean(acc, axis=-1, keepdims=True)
            xc = acc - mu
            y = xc * lax.rsqrt(jnp.mean(xc * xc, axis=-1, keepdims=True) + EPS)
            y = y * lng_ref[...] + lnb_ref[...]
            cat_ref[r0:r0 + CONV_ROWS, 0:CONV_CH] = (y * _sigmoid(y)).astype(BF16)
        rows = slice(part * part_rows, (part + 1) * part_rows)
        y = jnp.dot(cat_ref[rows, :], wo_ref[...], preferred_element_type=F32)
        o_ref[rows, :] = x_ref[rows, :] + _rms(y, gpost_ref[...])


def _mixer_out(x2, u, att, cw, cb, lng, lnb, wo, gpost, seq):
    t = x2.shape[0]
    const = lambda i: (0, 0)
    row = lambda i: (i, 0)
    halo_blocks = TM_OUT // CONV_HALO
    return pl.pallas_call(
        functools.partial(_mixer_out_kernel, seq // TM_OUT),
        grid=(t // TM_OUT,),
        in_specs=[
            pl.BlockSpec((TM_OUT, D_MODEL), row),
            pl.BlockSpec((TM_OUT, CONV_CH), row),
            pl.BlockSpec((CONV_HALO, CONV_CH), lambda i: (jnp.maximum(i * halo_blocks - 1, 0), 0)),
            pl.BlockSpec((TM_OUT, FOX_WIDTH), row),
            pl.BlockSpec((CONV_WIDTH, SUBLANES, CONV_CH), lambda i: (0, 0, 0)),
            pl.BlockSpec((1, CONV_CH), const),
            pl.BlockSpec((1, CONV_CH), const),
            pl.BlockSpec((1, CONV_CH), const),
            pl.BlockSpec((CONV_CH + FOX_WIDTH, D_MODEL), const),
            pl.BlockSpec((1, D_MODEL), const),
        ],
        out_specs=pl.BlockSpec((TM_OUT, D_MODEL), row),
        out_shape=jax.ShapeDtypeStruct((t, D_MODEL), F32),
        scratch_shapes=[
            pltpu.VMEM((SUBLANES, CONV_HALO + TM_OUT, CONV_CH), F32),
            pltpu.VMEM((TM_OUT, CONV_CH + FOX_WIDTH), BF16),
        ],
        compiler_params=pltpu.CompilerParams(
            dimension_semantics=("arbitrary",), vmem_limit_bytes=56 * MIB),
        name="mixer_out",
    )(x2, u, u, att, cw, cb, lng, lnb, wo, gpost)


def _mem_kv_kernel(mem_ref, g_ref, wk_ref, wv_ref, k_ref, v_ref):
    mn = _rms(mem_ref[...], g_ref[...]).astype(BF16)
    k_ref[...] = jnp.dot(mn, wk_ref[...], preferred_element_type=F32).astype(BF16)
    v_ref[...] = jnp.dot(mn, wv_ref[...], preferred_element_type=F32).astype(BF16)


def _mem_kv(mem2, gain, wk, wv, batch):
    const = lambda b: (0, 0)
    row = lambda b: (b, 0)
    return pl.pallas_call(
        _mem_kv_kernel,
        grid=(batch,),
        in_specs=[
            pl.BlockSpec((N_MEM, D_MODEL), row),
            pl.BlockSpec((1, D_MODEL), const),
            pl.BlockSpec((D_MODEL, MEM_INNER), const),
            pl.BlockSpec((D_MODEL, MEM_INNER), const),
        ],
        out_specs=[pl.BlockSpec((N_MEM, MEM_INNER), row), pl.BlockSpec((N_MEM, MEM_INNER), row)],
        out_shape=[jax.ShapeDtypeStruct((batch * N_MEM, MEM_INNER), BF16)] * 2,
        compiler_params=pltpu.CompilerParams(dimension_semantics=("arbitrary",)),
        name="mem_kv",
    )(mem2, gain, wk, wv)


def _mem_attn_kernel(x_ref, gpre_ref, wq_ref, km_ref, vm_ref, wo_ref, gpost_ref, o_ref, cat_ref):
    part_rows = TM_MEM // MEM_SPLIT
    for part in range(MEM_SPLIT):
        rows = slice(part * part_rows, (part + 1) * part_rows)
        x = x_ref[rows, :]
        h = _rms(x, gpre_ref[...]).astype(BF16)
        q = jnp.dot(h, wq_ref[...], preferred_element_type=F32)
        for hd in range(MEM_HEADS):
            lanes = slice(hd * MEM_HEAD_DIM, (hd + 1) * MEM_HEAD_DIM)
            s = lax.dot_general(q[:, lanes].astype(BF16), km_ref[:, lanes], (((1,), (1,)), ((), ())),
                                preferred_element_type=F32) * (MEM_HEAD_DIM ** -0.5)
            p = jnp.exp(s - jnp.max(s, axis=-1, keepdims=True))
            l = jnp.sum(p, axis=-1, keepdims=True)
            o = jnp.dot(p.astype(BF16), vm_ref[:, lanes], preferred_element_type=F32)
            cat_ref[rows, lanes] = (o / l).astype(BF16)
        y = jnp.dot(cat_ref[rows, :], wo_ref[...], preferred_element_type=F32)
        o_ref[rows, :] = x + _rms(y, gpost_ref[...])


def _mem_attn(x2, gpre, wq, km, vm, wo, gpost, seq):
    t = x2.shape[0]
    tiles_per_seq = seq // TM_MEM
    const = lambda i: (0, 0)
    row = lambda i: (i, 0)
    per_batch = lambda i: (i // tiles_per_seq, 0)
    return pl.pallas_call(
        _mem_attn_kernel,
        grid=(t // TM_MEM,),
        in_specs=[
            pl.BlockSpec((TM_MEM, D_MODEL), row),
            pl.BlockSpec((1, D_MODEL), const),
            pl.BlockSpec((D_MODEL, MEM_INNER), const),
            pl.BlockSpec((N_MEM, MEM_INNER), per_batch),
            pl.BlockSpec((N_MEM, MEM_INNER), per_batch),
            pl.BlockSpec((MEM_INNER, D_MODEL), const),
            pl.BlockSpec((1, D_MODEL), const),
        ],
        out_specs=pl.BlockSpec((TM_MEM, D_MODEL), row),
        out_shape=jax.ShapeDtypeStruct((t, D_MODEL), F32),
        scratch_shapes=[pltpu.VMEM((TM_MEM, MEM_INNER), BF16)],
        compiler_params=pltpu.CompilerParams(
            dimension_semantics=("arbitrary",), vmem_limit_bytes=32 * MIB),
        name="mem_attn",
    )(x2, gpre, wq, km, vm, wo, gpost)


def _mlp_kernel(x_ref, gpre_ref, wup_ref, wdn_ref, gpost_ref, o_ref):
    part_rows = TM_MLP // MLP_SPLIT
    for part in range(MLP_SPLIT):
        rows = slice(part * part_rows, (part + 1) * part_rows)
        x = x_ref[rows, :]
        h = _rms(x, gpre_ref[...]).astype(BF16)
        y = jnp.zeros((part_rows, D_MODEL), F32)
        for c in range(D_FF // FF_CHUNK):
            cols = slice(c * FF_CHUNK, (c + 1) * FF_CHUNK)
            a = jnp.maximum(jnp.dot(h, wup_ref[:, cols], preferred_element_type=F32), 0.0)
            y = y + jnp.dot((a * a).astype(BF16), wdn_ref[cols, :], preferred_element_type=F32)
        o_ref[rows, :] = x + _rms(y, gpost_ref[...])


def _mlp(x2, gpre, wup, wdn, gpost):
    t = x2.shape[0]
    const = lambda i: (0, 0)
    row = lambda i: (i, 0)
    return pl.pallas_call(
        _mlp_kernel,
        grid=(t // TM_MLP,),
        in_specs=[
            pl.BlockSpec((TM_MLP, D_MODEL), row),
            pl.BlockSpec((1, D_MODEL), const),
            pl.BlockSpec((D_MODEL, D_FF), const, pipeline_mode=pl.Buffered(1)),
            pl.BlockSpec((D_FF, D_MODEL), const, pipeline_mode=pl.Buffered(1)),
            pl.BlockSpec((1, D_MODEL), const),
        ],
        out_specs=pl.BlockSpec((TM_MLP, D_MODEL), row),
        out_shape=jax.ShapeDtypeStruct((t, D_MODEL), F32),
        compiler_params=pltpu.CompilerParams(
            dimension_semantics=("arbitrary",), vmem_limit_bytes=56 * MIB),
        name="mlp",
    )(x2, gpre, wup, wdn, gpost)


def kernel(x, mem, norm_mix_pre, norm_mix_post, w_in, b_forget, conv_w, conv_b, conv_ln_g, conv_ln_b, w_out, norm_mem_pre, norm_mem_post, norm_memkv, w_mq, w_mk, w_mv, w_mo, norm_mlp_pre, norm_mlp_post, w_up, w_down):
    batch, seq, d = x.shape
    depth = w_in.shape[0]
    assert d == D_MODEL and mem.shape[1] == N_MEM
    assert all(seq % tile == 0 for tile in (TM_IN * IN_SPLIT, TM_OUT, TQ, TM_MEM))
    x2 = x.reshape(batch * seq, d)
    mem2 = mem.reshape(batch * N_MEM, d)
    tri = (jnp.arange(TM_IN)[:, None] >= jnp.arange(TM_IN)[None, :]).astype(BF16)
    row = lambda a: a.reshape(1, -1)
    f_pad = LANES - FOX_HEADS
    for l in range(depth):
        wag = w_in[l, :, :2 * CONV_CH].astype(BF16)
        wqkv = jnp.pad(w_in[l, :, 2 * CONV_CH:], ((0, 0), (0, f_pad))).astype(BF16)
        bfp = jnp.pad(b_forget[l], (0, f_pad)).reshape(1, LANES)
        u, q, k, v, ccol, crow = _mixer_in(x2, row(norm_mix_pre[l]), wag, wqkv, bfp, tri, seq)
        att = _fox_attn(q, k, v, ccol, crow, batch, seq)
        cw8 = jnp.broadcast_to(conv_w[l][:, None, :], (CONV_WIDTH, SUBLANES, CONV_CH))
        x2 = _mixer_out(x2, u, att, cw8, row(conv_b[l]), row(conv_ln_g[l]),
                        row(conv_ln_b[l]), w_out[l].astype(BF16), row(norm_mix_post[l]), seq)
        km, vm = _mem_kv(mem2, row(norm_memkv[l]), w_mk[l].astype(BF16), w_mv[l].astype(BF16), batch)
        x2 = _mem_attn(x2, row(norm_mem_pre[l]), w_mq[l].astype(BF16), km, vm,
                       w_mo[l].astype(BF16), row(norm_mem_post[l]), seq)
        x2 = _mlp(x2, row(norm_mlp_pre[l]), w_up[l].astype(BF16), w_down[l].astype(BF16),
                  row(norm_mlp_post[l]))
    return x2.reshape(batch, seq, d)
```

```python
import functools

import jax
import jax.numpy as jnp
from jax import lax
from jax.experimental import pallas as pl
from jax.experimental.pallas import tpu as pltpu

D_MODEL = 1024
CONV_CH = 512
CONV_WIDTH = 31
FOX_HEADS = 8
FOX_HEAD_DIM = 64
FOX_WIDTH = FOX_HEADS * FOX_HEAD_DIM
N_MEM = 256
MEM_HEADS = 4
MEM_HEAD_DIM = 128
MEM_INNER = MEM_HEADS * MEM_HEAD_DIM
D_FF = 4 * D_MODEL
EPS = 1e-6
NEG_INF = -1e30
LOG2E = 1.4426950408889634

LANES = 128
SUBLANES = 8
MIB = 1024 * 1024

TM_IN = 512
IN_SPLIT = 2
TM_OUT = 1024
OUT_SPLIT = 4
CONV_HALO = 32
CONV_ROWS = 16
TQ = 256
TK = 256
HEAD_GROUP = 4
GROUP_W = HEAD_GROUP * FOX_HEAD_DIM
V_SLABS_W = FOX_HEADS * LANES
KV_UNROLLS = (8, 4, 2, 1)
TM_MEM = 1024
MEM_SPLIT = 2
TM_MLP = 1024
MLP_SPLIT = 2
FF_CHUNK = 1024

BF16 = jnp.bfloat16
F32 = jnp.float32


def _rms(x, gain):
    return x * lax.rsqrt(jnp.mean(x * x, axis=-1, keepdims=True) + EPS) * gain


def _sigmoid(x):
    return 1.0 / (1.0 + jnp.exp(-x))


def _log_sigmoid(x):
    return -(jnp.maximum(-x, 0.0) + jnp.log1p(jnp.exp(-jnp.abs(x))))


def _split3(x):
    hi = x.astype(BF16)
    r1 = x - hi.astype(F32)
    mid = r1.astype(BF16)
    lo = (r1 - mid.astype(F32)).astype(BF16)
    return hi, mid, lo


def _mixer_in_kernel(tiles_per_seq, x_ref, g_ref, wag_ref, wqkv_ref, bf_ref, tri_ref,
                     u_ref, q_ref, k_ref, v_ref, ccol_ref, crow_ref, carry_ref):
    i = pl.program_id(0)

    @pl.when(i % tiles_per_seq == 0)
    def _():
        carry_ref[...] = jnp.zeros_like(carry_ref)

    lane = lax.broadcasted_iota(jnp.int32, (TM_IN, LANES), 1)
    carry = carry_ref[0:1, :]
    for part in range(IN_SPLIT):
        rows = slice(part * TM_IN, (part + 1) * TM_IN)
        h = _rms(x_ref[rows, :], g_ref[...]).astype(BF16)
        ag = jnp.dot(h, wag_ref[...], preferred_element_type=F32)
        u_ref[rows, :] = ag[:, :CONV_CH] * _sigmoid(ag[:, CONV_CH:])
        qkv = jnp.dot(h, wqkv_ref[...], preferred_element_type=F32)
        q_ref[rows, :] = (qkv[:, :FOX_WIDTH] * (FOX_HEAD_DIM ** -0.5 * LOG2E)).astype(BF16)
        k_ref[rows, :] = qkv[:, FOX_WIDTH:2 * FOX_WIDTH].astype(BF16)
        for hd in range(FOX_HEADS):
            pair = qkv[:, 2 * FOX_WIDTH + (hd // 2) * LANES:2 * FOX_WIDTH + (hd // 2 + 1) * LANES]
            own = (lane < FOX_HEAD_DIM) if hd % 2 == 0 else (lane >= FOX_HEAD_DIM)
            v_ref[rows, hd * LANES:(hd + 1) * LANES] = jnp.where(own, pair, 1.0).astype(BF16)
        f = qkv[:, 3 * FOX_WIDTH:] + bf_ref[...]
        logf = _log_sigmoid(f)
        tri = tri_ref[...]
        parts = jnp.dot(tri, jnp.concatenate(_split3(logf), axis=1), preferred_element_type=F32)
        cum = (parts[:, :LANES] + parts[:, LANES:2 * LANES] + parts[:, 2 * LANES:]) + carry
        carry = cum[TM_IN - 1:TM_IN, :]
        cum2 = cum * LOG2E
        ccol_ref[rows, :] = cum2
        crow_ref[:, rows] = cum2.T[:FOX_HEADS, :]
    carry_ref[...] = jnp.broadcast_to(carry, carry_ref.shape)


def _mixer_in(x2, gain, wag, wqkv, bfp, tri, seq):
    t = x2.shape[0]
    const = lambda i: (0, 0)
    row = lambda i: (i, 0)
    tm = TM_IN * IN_SPLIT
    return pl.pallas_call(
        functools.partial(_mixer_in_kernel, seq // tm),
        grid=(t // tm,),
        in_specs=[
            pl.BlockSpec((tm, D_MODEL), row),
            pl.BlockSpec((1, D_MODEL), const),
            pl.BlockSpec((D_MODEL, 2 * CONV_CH), const),
            pl.BlockSpec((D_MODEL, 3 * FOX_WIDTH + LANES), const),
            pl.BlockSpec((1, LANES), const),
            pl.BlockSpec((TM_IN, TM_IN), const),
        ],
        out_specs=[
            pl.BlockSpec((tm, CONV_CH), row),
            pl.BlockSpec((tm, FOX_WIDTH), row),
            pl.BlockSpec((tm, FOX_WIDTH), row),
            pl.BlockSpec((tm, V_SLABS_W), row),
            pl.BlockSpec((tm, LANES), row),
            pl.BlockSpec((FOX_HEADS, tm), lambda i: (0, i)),
        ],
        out_shape=[
            jax.ShapeDtypeStruct((t, CONV_CH), F32),
            jax.ShapeDtypeStruct((t, FOX_WIDTH), BF16),
            jax.ShapeDtypeStruct((t, FOX_WIDTH), BF16),
            jax.ShapeDtypeStruct((t, V_SLABS_W), BF16),
            jax.ShapeDtypeStruct((t, LANES), F32),
            jax.ShapeDtypeStruct((FOX_HEADS, t), F32),
        ],
        scratch_shapes=[pltpu.VMEM((SUBLANES, LANES), F32)],
        compiler_params=pltpu.CompilerParams(
            dimension_semantics=("arbitrary",), vmem_limit_bytes=48 * MIB),
        name="mixer_in",
    )(x2, gain, wag, wqkv, bfp, tri)


def _fox_attn_kernel(q_ref, k_ref, v_ref, ccol_ref, crow_ref, hmask_ref, spread_ref, o_ref,
                     qcat_ref, s_ref, mx_ref, r_ref, acc_ref):
    i = pl.program_id(1)
    n_groups = FOX_HEADS // HEAD_GROUP
    for g in range(n_groups):
        qg = q_ref[:, g * GROUP_W:(g + 1) * GROUP_W]
        for h in range(HEAD_GROUP):
            qcat_ref[g, h * TQ:(h + 1) * TQ, :] = qg * hmask_ref[h:h + 1, :]
    mx_ref[...] = jnp.full(mx_ref.shape, NEG_INF, F32)

    def scores(j0, n_blocks):
        for u in range(n_blocks):
            ks = pl.multiple_of((j0 + u) * TK, TK)
            for g in range(n_groups):
                kj = k_ref[pl.ds(ks, TK), g * GROUP_W:(g + 1) * GROUP_W]
                s = lax.dot_general(qcat_ref[g], kj, (((1,), (1,)), ((), ())),
                                    preferred_element_type=F32)
                for h in range(HEAD_GROUP):
                    hg = g * HEAD_GROUP + h
                    t = s[h * TQ:(h + 1) * TQ, :] - crow_ref[hg:hg + 1, pl.ds(ks, TK)]
                    s_ref[hg, :, pl.ds(ks, TK)] = t
                    m = mx_ref[hg]
                    for c in range(TK // LANES):
                        m = jnp.maximum(m, t[:, c * LANES:(c + 1) * LANES])
                    mx_ref[hg] = m

    def loop_blocks(n, step):
        start = 0
        for width in KV_UNROLLS:
            trips = (n - start) // width

            def body(jj, carry, width=width, start=start):
                step(start + jj * width, width)
                return carry

            lax.fori_loop(0, trips, body, 0)
            start = start + trips * width

    loop_blocks(i, scores)

    kd = pl.multiple_of(i * TK, TK)
    causal = (lax.broadcasted_iota(jnp.int32, (TQ, TK), 1)
              <= lax.broadcasted_iota(jnp.int32, (TQ, TK), 0))
    cq_rep = jnp.dot(ccol_ref[...].astype(BF16), spread_ref[...], preferred_element_type=F32)
    for g in range(n_groups):
        s = lax.dot_general(qcat_ref[g], k_ref[pl.ds(kd, TK), g * GROUP_W:(g + 1) * GROUP_W],
                            (((1,), (1,)), ((), ())), preferred_element_type=F32)
        for h in range(HEAD_GROUP):
            hg = g * HEAD_GROUP + h
            t = jnp.where(causal, s[h * TQ:(h + 1) * TQ, :] - crow_ref[hg:hg + 1, pl.ds(kd, TK)],
                          NEG_INF)
            m_lane = mx_ref[hg]
            for c in range(TK // LANES):
                m_lane = jnp.maximum(m_lane, t[:, c * LANES:(c + 1) * LANES])
            cq = cq_rep[:, hg * LANES:(hg + 1) * LANES]
            m = jnp.max(m_lane, axis=-1, keepdims=True) + cq
            r = cq - m
            r_ref[hg] = r
            ps = [jnp.exp2(t[:, c * LANES:(c + 1) * LANES] + r).astype(BF16)
                  for c in range(TK // LANES)]
            acc_ref[hg] = jnp.dot(jnp.concatenate(ps, axis=1),
                                  v_ref[pl.ds(kd, TK), hg * LANES:(hg + 1) * LANES],
                                  preferred_element_type=F32)

    def pv(j0, n_blocks):
        ks = pl.multiple_of(j0 * TK, TK)
        for hg in range(FOX_HEADS):
            r = r_ref[hg]
            acc = acc_ref[hg]
            for b in range(n_blocks):
                ps = [jnp.exp2(s_ref[hg, :, pl.ds(ks + b * TK + c * LANES, LANES)] + r).astype(BF16)
                      for c in range(TK // LANES)]
                acc = acc + jnp.dot(jnp.concatenate(ps, axis=1),
                                    v_ref[pl.ds(ks + b * TK, TK), hg * LANES:(hg + 1) * LANES],
                                    preferred_element_type=F32)
            acc_ref[hg] = acc

    loop_blocks(i, pv)

    first_half = lax.broadcasted_iota(jnp.int32, (TQ, LANES), 1) < FOX_HEAD_DIM
    for pair in range(FOX_HEADS // 2):
        even = acc_ref[2 * pair]
        odd = acc_ref[2 * pair + 1]
        num = jnp.where(first_half, even, odd)
        den = pltpu.roll(jnp.where(first_half, odd, even), FOX_HEAD_DIM, axis=1)
        o_ref[:, pair * LANES:(pair + 1) * LANES] = (num / den).astype(BF16)


def _fox_attn(q, k, v, ccol, crow, batch, seq):
    t = q.shape[0]
    nq = seq // TQ
    hmask = (jnp.arange(GROUP_W)[None, :] // FOX_HEAD_DIM == jnp.arange(HEAD_GROUP)[:, None]).astype(BF16)
    spread = (jnp.arange(V_SLABS_W)[None, :] // LANES == jnp.arange(LANES)[:, None]).astype(BF16)
    qrow = lambda b, i: (b * nq + i, 0)
    whole_seq = lambda b, i: (b, 0)
    return pl.pallas_call(
        _fox_attn_kernel,
        grid=(batch, nq),
        in_specs=[
            pl.BlockSpec((TQ, FOX_WIDTH), qrow),
            pl.BlockSpec((seq, FOX_WIDTH), whole_seq),
            pl.BlockSpec((seq, V_SLABS_W), whole_seq, pipeline_mode=pl.Buffered(1)),
            pl.BlockSpec((TQ, LANES), qrow),
            pl.BlockSpec((FOX_HEADS, seq), lambda b, i: (0, b)),
            pl.BlockSpec((HEAD_GROUP, GROUP_W), lambda b, i: (0, 0)),
            pl.BlockSpec((LANES, V_SLABS_W), lambda b, i: (0, 0)),
        ],
        out_specs=pl.BlockSpec((TQ, FOX_WIDTH), qrow),
        out_shape=jax.ShapeDtypeStruct((t, FOX_WIDTH), BF16),
        scratch_shapes=[
            pltpu.VMEM((FOX_HEADS // HEAD_GROUP, HEAD_GROUP * TQ, GROUP_W), BF16),
            pltpu.VMEM((FOX_HEADS, TQ, seq), F32),
            pltpu.VMEM((FOX_HEADS, TQ, LANES), F32),
            pltpu.VMEM((FOX_HEADS, TQ, LANES), F32),
            pltpu.VMEM((FOX_HEADS, TQ, LANES), F32),
        ],
        compiler_params=pltpu.CompilerParams(
            dimension_semantics=("arbitrary", "arbitrary"), vmem_limit_bytes=58 * MIB),
        name="fox_attn",
    )(q, k, v, ccol, crow, hmask, spread)


def _mixer_out_kernel(tiles_per_seq, x_ref, u_ref, uprev_ref, att_ref, cw_ref, cb_ref, lng_ref,
                      lnb_ref, wo_ref, gpost_ref, o_ref, ubuf_ref, cat_ref):
    i = pl.program_id(0)
    first = (i % tiles_per_seq == 0)
    ubuf_ref[0, 0:CONV_HALO, :] = jnp.where(first, 0.0, uprev_ref[...])
    ubuf_ref[0, CONV_HALO:, :] = u_ref[...]
    shifted_rows = CONV_HALO + TM_OUT - SUBLANES
    for s in range(1, SUBLANES):
        ubuf_ref[s, 0:shifted_rows, :] = ubuf_ref[0, s:s + shifted_rows, :]
    base = CONV_HALO - (CONV_WIDTH - 1)
    cat_ref[:, CONV_CH:] = att_ref[...]
    part_rows = TM_OUT // OUT_SPLIT
    for part in range(OUT_SPLIT):
        for c in range(part_rows // CONV_ROWS):
            r0 = part * part_rows + c * CONV_ROWS
            accs = [jnp.broadcast_to(cb_ref[...], (SUBLANES, CONV_CH))] * (CONV_ROWS // SUBLANES)
            for kk in range(CONV_WIDTH):
                s = (base + kk) % SUBLANES
                a = r0 + base + kk - s
                w8 = cw_ref[kk]
                accs = [acc + w8 * ubuf_ref[s, a + g * SUBLANES:a + (g + 1) * SUBLANES, :]
                        for g, acc in enumerate(accs)]
            acc = jnp.concatenate(accs, axis=0)
            mu = jnp.mean(acc, axis=-1, keepdims=True)
            xc = acc - mu
            y = xc * lax.rsqrt(jnp.mean(xc * xc, axis=-1, keepdims=True) + EPS)
            y = y * lng_ref[...] + lnb_ref[...]
            cat_ref[r0:r0 + CONV_ROWS, 0:CONV_CH] = (y * _sigmoid(y)).astype(BF16)
        rows = slice(part * part_rows, (part + 1) * part_rows)
        y = jnp.dot(cat_ref[rows, :], wo_ref[...], preferred_element_type=F32)
        o_ref[rows, :] = x_ref[rows, :] + _rms(y, gpost_ref[...])


def _mixer_out(x2, u, att, cw, cb, lng, lnb, wo, gpost, seq):
    t = x2.shape[0]
    const = lambda i: (0, 0)
    row = lambda i: (i, 0)
    halo_blocks = TM_OUT // CONV_HALO
    return pl.pallas_call(
        functools.partial(_mixer_out_kernel, seq // TM_OUT),
        grid=(t // TM_OUT,),
        in_specs=[
            pl.BlockSpec((TM_OUT, D_MODEL), row),
            pl.BlockSpec((TM_OUT, CONV_CH), row),
            pl.BlockSpec((CONV_HALO, CONV_CH), lambda i: (jnp.maximum(i * halo_blocks - 1, 0), 0)),
            pl.BlockSpec((TM_OUT, FOX_WIDTH), row),
            pl.BlockSpec((CONV_WIDTH, SUBLANES, CONV_CH), lambda i: (0, 0, 0)),
            pl.BlockSpec((1, CONV_CH), const),
            pl.BlockSpec((1, CONV_CH), const),
            pl.BlockSpec((1, CONV_CH), const),
            pl.BlockSpec((CONV_CH + FOX_WIDTH, D_MODEL), const),
            pl.BlockSpec((1, D_MODEL), const),
        ],
        out_specs=pl.BlockSpec((TM_OUT, D_MODEL), row),
        out_shape=jax.ShapeDtypeStruct((t, D_MODEL), F32),
        scratch_shapes=[
            pltpu.VMEM((SUBLANES, CONV_HALO + TM_OUT, CONV_CH), F32),
            pltpu.VMEM((TM_OUT, CONV_CH + FOX_WIDTH), BF16),
        ],
        compiler_params=pltpu.CompilerParams(
            dimension_semantics=("arbitrary",), vmem_limit_bytes=56 * MIB),
        name="mixer_out",
    )(x2, u, u, att, cw, cb, lng, lnb, wo, gpost)


def _mem_kv_kernel(mem_ref, g_ref, wk_ref, wv_ref, k_ref, v_ref):
    mn = _rms(mem_ref[...], g_ref[...]).astype(BF16)
    k_ref[...] = jnp.dot(mn, wk_ref[...], preferred_element_type=F32).astype(BF16)
    v_ref[...] = jnp.dot(mn, wv_ref[...], preferred_element_type=F32).astype(BF16)


def _mem_kv(mem2, gain, wk, wv, batch):
    const = lambda b: (0, 0)
    row = lambda b: (b, 0)
    return pl.pallas_call(
        _mem_kv_kernel,
        grid=(batch,),
        in_specs=[
            pl.BlockSpec((N_MEM, D_MODEL), row),
            pl.BlockSpec((1, D_MODEL), const),
            pl.BlockSpec((D_MODEL, MEM_INNER), const),
            pl.BlockSpec((D_MODEL, MEM_INNER), const),
        ],
        out_specs=[pl.BlockSpec((N_MEM, MEM_INNER), row), pl.BlockSpec((N_MEM, MEM_INNER), row)],
        out_shape=[jax.ShapeDtypeStruct((batch * N_MEM, MEM_INNER), BF16)] * 2,
        compiler_params=pltpu.CompilerParams(dimension_semantics=("arbitrary",)),
        name="mem_kv",
    )(mem2, gain, wk, wv)


def _mem_attn_kernel(x_ref, gpre_ref, wq_ref, km_ref, vm_ref, wo_ref, gpost_ref, o_ref, cat_ref):
    part_rows = TM_MEM // MEM_SPLIT
    for part in range(MEM_SPLIT):
        rows = slice(part * part_rows, (part + 1) * part_rows)
        x = x_ref[rows, :]
        h = _rms(x, gpre_ref[...]).astype(BF16)
        q = jnp.dot(h, wq_ref[...], preferred_element_type=F32)
        for hd in range(MEM_HEADS):
            lanes = slice(hd * MEM_HEAD_DIM, (hd + 1) * MEM_HEAD_DIM)
            s = lax.dot_general(q[:, lanes].astype(BF16), km_ref[:, lanes], (((1,), (1,)), ((), ())),
                                preferred_element_type=F32) * (MEM_HEAD_DIM ** -0.5)
            p = jnp.exp(s - jnp.max(s, axis=-1, keepdims=True))
            l = jnp.sum(p, axis=-1, keepdims=True)
            o = jnp.dot(p.astype(BF16), vm_ref[:, lanes], preferred_element_type=F32)
            cat_ref[rows, lanes] = (o / l).astype(BF16)
        y = jnp.dot(cat_ref[rows, :], wo_ref[...], preferred_element_type=F32)
        o_ref[rows, :] = x + _rms(y, gpost_ref[...])


def _mem_attn(x2, gpre, wq, km, vm, wo, gpost, seq):
    t = x2.shape[0]
    tiles_per_seq = seq // TM_MEM
    const = lambda i: (0, 0)
    row = lambda i: (i, 0)
    per_batch = lambda i: (i // tiles_per_seq, 0)
    return pl.pallas_call(
        _mem_attn_kernel,
        grid=(t // TM_MEM,),
        in_specs=[
            pl.BlockSpec((TM_MEM, D_MODEL), row),
            pl.BlockSpec((1, D_MODEL), const),
            pl.BlockSpec((D_MODEL, MEM_INNER), const),
            pl.BlockSpec((N_MEM, MEM_INNER), per_batch),
            pl.BlockSpec((N_MEM, MEM_INNER), per_batch),
            pl.BlockSpec((MEM_INNER, D_MODEL), const),
            pl.BlockSpec((1, D_MODEL), const),
        ],
        out_specs=pl.BlockSpec((TM_MEM, D_MODEL), row),
        out_shape=jax.ShapeDtypeStruct((t, D_MODEL), F32),
        scratch_shapes=[pltpu.VMEM((TM_MEM, MEM_INNER), BF16)],
        compiler_params=pltpu.CompilerParams(
            dimension_semantics=("arbitrary",), vmem_limit_bytes=32 * MIB),
        name="mem_attn",
    )(x2, gpre, wq, km, vm, wo, gpost)


def _mlp_kernel(x_ref, gpre_ref, wup_ref, wdn_ref, gpost_ref, o_ref):
    part_rows = TM_MLP // MLP_SPLIT
    for part in range(MLP_SPLIT):
        rows = slice(part * part_rows, (part + 1) * part_rows)
        x = x_ref[rows, :]
        h = _rms(x, gpre_ref[...]).astype(BF16)
        y = jnp.zeros((part_rows, D_MODEL), F32)
        for c in range(D_FF // FF_CHUNK):
            cols = slice(c * FF_CHUNK, (c + 1) * FF_CHUNK)
            a = jnp.maximum(jnp.dot(h, wup_ref[:, cols], preferred_element_type=F32), 0.0)
            y = y + jnp.dot((a * a).astype(BF16), wdn_ref[cols, :], preferred_element_type=F32)
        o_ref[rows, :] = x + _rms(y, gpost_ref[...])


def _mlp(x2, gpre, wup, wdn, gpost):
    t = x2.shape[0]
    const = lambda i: (0, 0)
    row = lambda i: (i, 0)
    return pl.pallas_call(
        _mlp_kernel,
        grid=(t // TM_MLP,),
        in_specs=[
            pl.BlockSpec((TM_MLP, D_MODEL), row),
            pl.BlockSpec((1, D_MODEL), const),
            pl.BlockSpec((D_MODEL, D_FF), const, pipeline_mode=pl.Buffered(1)),
            pl.BlockSpec((D_FF, D_MODEL), const, pipeline_mode=pl.Buffered(1)),
            pl.BlockSpec((1, D_MODEL), const),
        ],
        out_specs=pl.BlockSpec((TM_MLP, D_MODEL), row),
        out_shape=jax.ShapeDtypeStruct((t, D_MODEL), F32),
        compiler_params=pltpu.CompilerParams(
            dimension_semantics=("arbitrary",), vmem_limit_bytes=56 * MIB),
        name="mlp",
    )(x2, gpre, wup, wdn, gpost)


def kernel(x, mem, norm_mix_pre, norm_mix_post, w_in, b_forget, conv_w, conv_b, conv_ln_g, conv_ln_b, w_out, norm_mem_pre, norm_mem_post, norm_memkv, w_mq, w_mk, w_mv, w_mo, norm_mlp_pre, norm_mlp_post, w_up, w_down):
    batch, seq, d = x.shape
    depth = w_in.shape[0]
    assert d == D_MODEL and mem.shape[1] == N_MEM
    assert all(seq % tile == 0 for tile in (TM_IN * IN_SPLIT, TM_OUT, TQ, TM_MEM))
    x2 = x.reshape(batch * seq, d)
    mem2 = mem.reshape(batch * N_MEM, d)
    tri = (jnp.arange(TM_IN)[:, None] >= jnp.arange(TM_IN)[None, :]).astype(BF16)
    row = lambda a: a.reshape(1, -1)
    f_pad = LANES - FOX_HEADS
    for l in range(depth):
        wag = w_in[l, :, :2 * CONV_CH].astype(BF16)
        wqkv = jnp.pad(w_in[l, :, 2 * CONV_CH:], ((0, 0), (0, f_pad))).astype(BF16)
        bfp = jnp.pad(b_forget[l], (0, f_pad)).reshape(1, LANES)
        u, q, k, v, ccol, crow = _mixer_in(x2, row(norm_mix_pre[l]), wag, wqkv, bfp, tri, seq)
        att = _fox_attn(q, k, v, ccol, crow, batch, seq)
        cw8 = jnp.broadcast_to(conv_w[l][:, None, :], (CONV_WIDTH, SUBLANES, CONV_CH))
        x2 = _mixer_out(x2, u, att, cw8, row(conv_b[l]), row(conv_ln_g[l]),
                        row(conv_ln_b[l]), w_out[l].astype(BF16), row(norm_mix_post[l]), seq)
        km, vm = _mem_kv(mem2, row(norm_memkv[l]), w_mk[l].astype(BF16), w_mv[l].astype(BF16), batch)
        x2 = _mem_attn(x2, row(norm_mem_pre[l]), w_mq[l].astype(BF16), km, vm,
                       w_mo[l].astype(BF16), row(norm_mem_post[l]), seq)
        x2 = _mlp(x2, row(norm_mlp_pre[l]), w_up[l].astype(BF16), w_down[l].astype(BF16),
                  row(norm_mlp_post[l]))
    return x2.reshape(batch, seq, d)
```

```python
import functools

import jax
import jax.numpy as jnp
from jax import lax
from jax.experimental import pallas as pl
from jax.experimental.pallas import tpu as pltpu

D_MODEL = 1024
CONV_CH = 512
CONV_WIDTH = 31
FOX_HEADS = 8
FOX_HEAD_DIM = 64
FOX_WIDTH = FOX_HEADS * FOX_HEAD_DIM
N_MEM = 256
MEM_HEADS = 4
MEM_HEAD_DIM = 128
MEM_INNER = MEM_HEADS * MEM_HEAD_DIM
D_FF = 4 * D_MODEL
EPS = 1e-6
NEG_INF = -1e30
LOG2E = 1.4426950408889634

LANES = 128
SUBLANES = 8
MIB = 1024 * 1024

TM_IN = 512
IN_SPLIT = 2
TM_OUT = 1024
OUT_SPLIT = 4
CONV_HALO = 32
CONV_ROWS = 16
TQ = 256
TK = 256
Q_PER_STEP = 2
HEAD_GROUP = 4
GROUP_W = HEAD_GROUP * FOX_HEAD_DIM
V_SLABS_W = FOX_HEADS * LANES
KV_UNROLLS = (8, 4, 2, 1)
TM_MEM = 1024
MEM_SPLIT = 2
TM_MLP = 1024
MLP_SPLIT = 2
FF_CHUNK = 1024

BF16 = jnp.bfloat16
F32 = jnp.float32


def _rms(x, gain):
    return x * lax.rsqrt(jnp.mean(x * x, axis=-1, keepdims=True) + EPS) * gain


def _sigmoid(x):
    return 1.0 / (1.0 + jnp.exp(-x))


def _log_sigmoid(x):
    return -(jnp.maximum(-x, 0.0) + jnp.log1p(jnp.exp(-jnp.abs(x))))


def _split3(x):
    hi = x.astype(BF16)
    r1 = x - hi.astype(F32)
    mid = r1.astype(BF16)
    lo = (r1 - mid.astype(F32)).astype(BF16)
    return hi, mid, lo


def _mixer_in_kernel(tiles_per_seq, x_ref, g_ref, wag_ref, wqkv_ref, bf_ref, tri_ref,
                     u_ref, q_ref, k_ref, v_ref, ccol_ref, crow_ref, carry_ref):
    i = pl.program_id(0)

    @pl.when(i % tiles_per_seq == 0)
    def _():
        carry_ref[...] = jnp.zeros_like(carry_ref)

    lane = lax.broadcasted_iota(jnp.int32, (TM_IN, LANES), 1)
    carry = carry_ref[0:1, :]
    for part in range(IN_SPLIT):
        rows = slice(part * TM_IN, (part + 1) * TM_IN)
        h = _rms(x_ref[rows, :], g_ref[...]).astype(BF16)
        ag = jnp.dot(h, wag_ref[...], preferred_element_type=F32)
        u_ref[rows, :] = ag[:, :CONV_CH] * _sigmoid(ag[:, CONV_CH:])
        qkv = jnp.dot(h, wqkv_ref[...], preferred_element_type=F32)
        q_ref[rows, :] = (qkv[:, :FOX_WIDTH] * (FOX_HEAD_DIM ** -0.5 * LOG2E)).astype(BF16)
        k_ref[rows, :] = qkv[:, FOX_WIDTH:2 * FOX_WIDTH].astype(BF16)
        for hd in range(FOX_HEADS):
            pair = qkv[:, 2 * FOX_WIDTH + (hd // 2) * LANES:2 * FOX_WIDTH + (hd // 2 + 1) * LANES]
            own = (lane < FOX_HEAD_DIM) if hd % 2 == 0 else (lane >= FOX_HEAD_DIM)
            v_ref[rows, hd * LANES:(hd + 1) * LANES] = jnp.where(own, pair, 1.0).astype(BF16)
        f = qkv[:, 3 * FOX_WIDTH:] + bf_ref[...]
        logf = _log_sigmoid(f)
        tri = tri_ref[...]
        parts = jnp.dot(tri, jnp.concatenate(_split3(logf), axis=1), preferred_element_type=F32)
        cum = (parts[:, :LANES] + parts[:, LANES:2 * LANES] + parts[:, 2 * LANES:]) + carry
        carry = cum[TM_IN - 1:TM_IN, :]
        cum2 = cum * LOG2E
        ccol_ref[rows, :] = cum2
        crow_ref[:, rows] = cum2.T[:FOX_HEADS, :]
    carry_ref[...] = jnp.broadcast_to(carry, carry_ref.shape)


def _mixer_in(x2, gain, wag, wqkv, bfp, tri, seq):
    t = x2.shape[0]
    const = lambda i: (0, 0)
    row = lambda i: (i, 0)
    tm = TM_IN * IN_SPLIT
    return pl.pallas_call(
        functools.partial(_mixer_in_kernel, seq // tm),
        grid=(t // tm,),
        in_specs=[
            pl.BlockSpec((tm, D_MODEL), row),
            pl.BlockSpec((1, D_MODEL), const),
            pl.BlockSpec((D_MODEL, 2 * CONV_CH), const),
            pl.BlockSpec((D_MODEL, 3 * FOX_WIDTH + LANES), const),
            pl.BlockSpec((1, LANES), const),
            pl.BlockSpec((TM_IN, TM_IN), const),
        ],
        out_specs=[
            pl.BlockSpec((tm, CONV_CH), row),
            pl.BlockSpec((tm, FOX_WIDTH), row),
            pl.BlockSpec((tm, FOX_WIDTH), row),
            pl.BlockSpec((tm, V_SLABS_W), row),
            pl.BlockSpec((tm, LANES), row),
            pl.BlockSpec((FOX_HEADS, tm), lambda i: (0, i)),
        ],
        out_shape=[
            jax.ShapeDtypeStruct((t, CONV_CH), F32),
            jax.ShapeDtypeStruct((t, FOX_WIDTH), BF16),
            jax.ShapeDtypeStruct((t, FOX_WIDTH), BF16),
            jax.ShapeDtypeStruct((t, V_SLABS_W), BF16),
            jax.ShapeDtypeStruct((t, LANES), F32),
            jax.ShapeDtypeStruct((FOX_HEADS, t), F32),
        ],
        scratch_shapes=[pltpu.VMEM((SUBLANES, LANES), F32)],
        compiler_params=pltpu.CompilerParams(
            dimension_semantics=("arbitrary",), vmem_limit_bytes=48 * MIB),
        name="mixer_in",
    )(x2, gain, wag, wqkv, bfp, tri)


def _fox_attn_kernel(q_ref, k_ref, v_ref, ccol_ref, crow_ref, hmask_ref, spread_ref, o_ref,
                     qcat_ref, s_ref, mx_ref, r_ref, acc_ref):
    def one_block(qb, carry):
        rows = pl.ds(pl.multiple_of(qb * TQ, TQ), TQ)
        _attend_block(pl.program_id(1) * Q_PER_STEP + qb, q_ref.at[rows], k_ref, v_ref,
                      ccol_ref.at[rows], crow_ref, hmask_ref, spread_ref, o_ref.at[rows],
                      qcat_ref, s_ref, mx_ref, r_ref, acc_ref)
        return carry

    lax.fori_loop(0, Q_PER_STEP, one_block, 0)


def _attend_block(i, q_ref, k_ref, v_ref, ccol_ref, crow_ref, hmask_ref, spread_ref, o_ref,
                  qcat_ref, s_ref, mx_ref, r_ref, acc_ref):
    n_groups = FOX_HEADS // HEAD_GROUP
    for g in range(n_groups):
        qg = q_ref[:, g * GROUP_W:(g + 1) * GROUP_W]
        for h in range(HEAD_GROUP):
            qcat_ref[g, h * TQ:(h + 1) * TQ, :] = qg * hmask_ref[h:h + 1, :]
    mx_ref[...] = jnp.full(mx_ref.shape, NEG_INF, F32)

    def scores(j0, n_blocks):
        for u in range(n_blocks):
            ks = pl.multiple_of((j0 + u) * TK, TK)
            for g in range(n_groups):
                kj = k_ref[pl.ds(ks, TK), g * GROUP_W:(g + 1) * GROUP_W]
                s = lax.dot_general(qcat_ref[g], kj, (((1,), (1,)), ((), ())),
                                    preferred_element_type=F32)
                for h in range(HEAD_GROUP):
                    hg = g * HEAD_GROUP + h
                    t = s[h * TQ:(h + 1) * TQ, :] - crow_ref[hg:hg + 1, pl.ds(ks, TK)]
                    s_ref[hg, :, pl.ds(ks, TK)] = t
                    m = mx_ref[hg]
                    for c in range(TK // LANES):
                        m = jnp.maximum(m, t[:, c * LANES:(c + 1) * LANES])
                    mx_ref[hg] = m

    def loop_blocks(n, step):
        start = 0
        for width in KV_UNROLLS:
            trips = (n - start) // width

            def body(jj, carry, width=width, start=start):
                step(start + jj * width, width)
                return carry

            lax.fori_loop(0, trips, body, 0)
            start = start + trips * width

    loop_blocks(i, scores)

    kd = pl.multiple_of(i * TK, TK)
    causal = (lax.broadcasted_iota(jnp.int32, (TQ, TK), 1)
              <= lax.broadcasted_iota(jnp.int32, (TQ, TK), 0))
    cq_rep = jnp.dot(ccol_ref[...].astype(BF16), spread_ref[...], preferred_element_type=F32)
    for g in range(n_groups):
        s = lax.dot_general(qcat_ref[g], k_ref[pl.ds(kd, TK), g * GROUP_W:(g + 1) * GROUP_W],
                            (((1,), (1,)), ((), ())), preferred_element_type=F32)
        for h in range(HEAD_GROUP):
            hg = g * HEAD_GROUP + h
            t = jnp.where(causal, s[h * TQ:(h + 1) * TQ, :] - crow_ref[hg:hg + 1, pl.ds(kd, TK)],
                          NEG_INF)
            m_lane = mx_ref[hg]
            for c in range(TK // LANES):
                m_lane = jnp.maximum(m_lane, t[:, c * LANES:(c + 1) * LANES])
            cq = cq_rep[:, hg * LANES:(hg + 1) * LANES]
            m = jnp.max(m_lane, axis=-1, keepdims=True) + cq
            r = cq - m
            r_ref[hg] = r
            ps = [jnp.exp2(t[:, c * LANES:(c + 1) * LANES] + r).astype(BF16)
                  for c in range(TK // LANES)]
            acc_ref[hg] = jnp.dot(jnp.concatenate(ps, axis=1),
                                  v_ref[pl.ds(kd, TK), hg * LANES:(hg + 1) * LANES],
                                  preferred_element_type=F32)

    def pv(j0, n_blocks):
        ks = pl.multiple_of(j0 * TK, TK)
        for hg in range(FOX_HEADS):
            r = r_ref[hg]
            acc = acc_ref[hg]
            for b in range(n_blocks):
                ps = [jnp.exp2(s_ref[hg, :, pl.ds(ks + b * TK + c * LANES, LANES)] + r).astype(BF16)
                      for c in range(TK // LANES)]
                acc = acc + jnp.dot(jnp.concatenate(ps, axis=1),
                                    v_ref[pl.ds(ks + b * TK, TK), hg * LANES:(hg + 1) * LANES],
                                    preferred_element_type=F32)
            acc_ref[hg] = acc

    loop_blocks(i, pv)

    first_half = lax.broadcasted_iota(jnp.int32, (TQ, LANES), 1) < FOX_HEAD_DIM
    for pair in range(FOX_HEADS // 2):
        even = acc_ref[2 * pair]
        odd = acc_ref[2 * pair + 1]
        num = jnp.where(first_half, even, odd)
        den = pltpu.roll(jnp.where(first_half, odd, even), FOX_HEAD_DIM, axis=1)
        o_ref[:, pair * LANES:(pair + 1) * LANES] = (num / den).astype(BF16)


def _fox_attn(q, k, v, ccol, crow, batch, seq):
    t = q.shape[0]
    step_rows = TQ * Q_PER_STEP
    nq = seq // step_rows
    hmask = (jnp.arange(GROUP_W)[None, :] // FOX_HEAD_DIM == jnp.arange(HEAD_GROUP)[:, None]).astype(BF16)
    spread = (jnp.arange(V_SLABS_W)[None, :] // LANES == jnp.arange(LANES)[:, None]).astype(BF16)
    qrow = lambda b, i: (b * nq + i, 0)
    whole_seq = lambda b, i: (b, 0)
    return pl.pallas_call(
        _fox_attn_kernel,
        grid=(batch, nq),
        in_specs=[
            pl.BlockSpec((step_rows, FOX_WIDTH), qrow),
            pl.BlockSpec((seq, FOX_WIDTH), whole_seq),
            pl.BlockSpec((seq, V_SLABS_W), whole_seq, pipeline_mode=pl.Buffered(1)),
            pl.BlockSpec((step_rows, LANES), qrow),
            pl.BlockSpec((FOX_HEADS, seq), lambda b, i: (0, b)),
            pl.BlockSpec((HEAD_GROUP, GROUP_W), lambda b, i: (0, 0)),
            pl.BlockSpec((LANES, V_SLABS_W), lambda b, i: (0, 0)),
        ],
        out_specs=pl.BlockSpec((step_rows, FOX_WIDTH), qrow),
        out_shape=jax.ShapeDtypeStruct((t, FOX_WIDTH), BF16),
        scratch_shapes=[
            pltpu.VMEM((FOX_HEADS // HEAD_GROUP, HEAD_GROUP * TQ, GROUP_W), BF16),
            pltpu.VMEM((FOX_HEADS, TQ, seq), F32),
            pltpu.VMEM((FOX_HEADS, TQ, LANES), F32),
            pltpu.VMEM((FOX_HEADS, TQ, LANES), F32),
            pltpu.VMEM((FOX_HEADS, TQ, LANES), F32),
        ],
        compiler_params=pltpu.CompilerParams(
            dimension_semantics=("arbitrary", "arbitrary"), vmem_limit_bytes=58 * MIB),
        name="fox_attn",
    )(q, k, v, ccol, crow, hmask, spread)


def _mixer_out_kernel(tiles_per_seq, x_ref, u_ref, uprev_ref, att_ref, cw_ref, cb_ref, lng_ref,
                      lnb_ref, wo_ref, gpost_ref, o_ref, ubuf_ref, cat_ref):
    i = pl.program_id(0)
    first = (i % tiles_per_seq == 0)
    ubuf_ref[0, 0:CONV_HALO, :] = jnp.where(first, 0.0, uprev_ref[...])
    ubuf_ref[0, CONV_HALO:, :] = u_ref[...]
    shifted_rows = CONV_HALO + TM_OUT - SUBLANES
    for s in range(1, SUBLANES):
        ubuf_ref[s, 0:shifted_rows, :] = ubuf_ref[0, s:s + shifted_rows, :]
    base = CONV_HALO - (CONV_WIDTH - 1)
    cat_ref[:, CONV_CH:] = att_ref[...]
    part_rows = TM_OUT // OUT_SPLIT
    for part in range(OUT_SPLIT):
        for c in range(part_rows // CONV_ROWS):
            r0 = part * part_rows + c * CONV_ROWS
            accs = [jnp.broadcast_to(cb_ref[...], (SUBLANES, CONV_CH))] * (CONV_ROWS // SUBLANES)
            for kk in range(CONV_WIDTH):
                s = (base + kk) % SUBLANES
                a = r0 + base + kk - s
                w8 = cw_ref[kk]
                accs = [acc + w8 * ubuf_ref[s, a + g * SUBLANES:a + (g + 1) * SUBLANES, :]
                        for g, acc in enumerate(accs)]
            acc = jnp.concatenate(accs, axis=0)
            mu = jnp.mean(acc, axis=-1, keepdims=True)
            xc = acc - mu
            y = xc * lax.rsqrt(jnp.mean(xc * xc, axis=-1, keepdims=True) + EPS)
            y = y * lng_ref[...] + lnb_ref[...]
            cat_ref[r0:r0 + CONV_ROWS, 0:CONV_CH] = (y * _sigmoid(y)).astype(BF16)
        rows = slice(part * part_rows, (part + 1) * part_rows)
        y = jnp.dot(cat_ref[rows, :], wo_ref[...], preferred_element_type=F32)
        o_ref[rows, :] = x_ref[rows, :] + _rms(y, gpost_ref[...])


def _mixer_out(x2, u, att, cw, cb, lng, lnb, wo, gpost, seq):
    t = x2.shape[0]
    const = lambda i: (0, 0)
    row = lambda i: (i, 0)
    halo_blocks = TM_OUT // CONV_HALO
    return pl.pallas_call(
        functools.partial(_mixer_out_kernel, seq // TM_OUT),
        grid=(t // TM_OUT,),
        in_specs=[
            pl.BlockSpec((TM_OUT, D_MODEL), row),
            pl.BlockSpec((TM_OUT, CONV_CH), row),
            pl.BlockSpec((CONV_HALO, CONV_CH), lambda i: (jnp.maximum(i * halo_blocks - 1, 0), 0)),
            pl.BlockSpec((TM_OUT, FOX_WIDTH), row),
            pl.BlockSpec((CONV_WIDTH, SUBLANES, CONV_CH), lambda i: (0, 0, 0)),
            pl.BlockSpec((1, CONV_CH), const),
            pl.BlockSpec((1, CONV_CH), const),
            pl.BlockSpec((1, CONV_CH), const),
            pl.BlockSpec((CONV_CH + FOX_WIDTH, D_MODEL), const),
            pl.BlockSpec((1, D_MODEL), const),
        ],
        out_specs=pl.BlockSpec((TM_OUT, D_MODEL), row),
        out_shape=jax.ShapeDtypeStruct((t, D_MODEL), F32),
        scratch_shapes=[
            pltpu.VMEM((SUBLANES, CONV_HALO + TM_OUT, CONV_CH), F32),
            pltpu.VMEM((TM_OUT, CONV_CH + FOX_WIDTH), BF16),
        ],
        compiler_params=pltpu.CompilerParams(
            dimension_semantics=("arbitrary",), vmem_limit_bytes=56 * MIB),
        name="mixer_out",
    )(x2, u, u, att, cw, cb, lng, lnb, wo, gpost)


def _mem_kv_kernel(mem_ref, g_ref, wk_ref, wv_ref, k_ref, v_ref):
    mn = _rms(mem_ref[...], g_ref[...]).astype(BF16)
    k_ref[...] = jnp.dot(mn, wk_ref[...], preferred_element_type=F32).astype(BF16)
    v_ref[...] = jnp.dot(mn, wv_ref[...], preferred_element_type=F32).astype(BF16)


def _mem_kv(mem2, gain, wk, wv, batch):
    const = lambda b: (0, 0)
    row = lambda b: (b, 0)
    return pl.pallas_call(
        _mem_kv_kernel,
        grid=(batch,),
        in_specs=[
            pl.BlockSpec((N_MEM, D_MODEL), row),
            pl.BlockSpec((1, D_MODEL), const),
            pl.BlockSpec((D_MODEL, MEM_INNER), const),
            pl.BlockSpec((D_MODEL, MEM_INNER), const),
        ],
        out_specs=[pl.BlockSpec((N_MEM, MEM_INNER), row), pl.BlockSpec((N_MEM, MEM_INNER), row)],
        out_shape=[jax.ShapeDtypeStruct((batch * N_MEM, MEM_INNER), BF16)] * 2,
        compiler_params=pltpu.CompilerParams(dimension_semantics=("arbitrary",)),
        name="mem_kv",
    )(mem2, gain, wk, wv)


def _mem_attn_kernel(x_ref, gpre_ref, wq_ref, km_ref, vm_ref, wo_ref, gpost_ref, o_ref, cat_ref):
    part_rows = TM_MEM // MEM_SPLIT
    for part in range(MEM_SPLIT):
        rows = slice(part * part_rows, (part + 1) * part_rows)
        x = x_ref[rows, :]
        h = _rms(x, gpre_ref[...]).astype(BF16)
        q = jnp.dot(h, wq_ref[...], preferred_element_type=F32)
        for hd in range(MEM_HEADS):
            lanes = slice(hd * MEM_HEAD_DIM, (hd + 1) * MEM_HEAD_DIM)
            s = lax.dot_general(q[:, lanes].astype(BF16), km_ref[:, lanes], (((1,), (1,)), ((), ())),
                                preferred_element_type=F32) * (MEM_HEAD_DIM ** -0.5)
            p = jnp.exp(s - jnp.max(s, axis=-1, keepdims=True))
            l = jnp.sum(p, axis=-1, keepdims=True)
            o = jnp.dot(p.astype(BF16), vm_ref[:, lanes], preferred_element_type=F32)
            cat_ref[rows, lanes] = (o / l).astype(BF16)
        y = jnp.dot(cat_ref[rows, :], wo_ref[...], preferred_element_type=F32)
        o_ref[rows, :] = x + _rms(y, gpost_ref[...])


def _mem_attn(x2, gpre, wq, km, vm, wo, gpost, seq):
    t = x2.shape[0]
    tiles_per_seq = seq // TM_MEM
    const = lambda i: (0, 0)
    row = lambda i: (i, 0)
    per_batch = lambda i: (i // tiles_per_seq, 0)
    return pl.pallas_call(
        _mem_attn_kernel,
        grid=(t // TM_MEM,),
        in_specs=[
            pl.BlockSpec((TM_MEM, D_MODEL), row),
            pl.BlockSpec((1, D_MODEL), const),
            pl.BlockSpec((D_MODEL, MEM_INNER), const),
            pl.BlockSpec((N_MEM, MEM_INNER), per_batch),
            pl.BlockSpec((N_MEM, MEM_INNER), per_batch),
            pl.BlockSpec((MEM_INNER, D_MODEL), const),
            pl.BlockSpec((1, D_MODEL), const),
        ],
        out_specs=pl.BlockSpec((TM_MEM, D_MODEL), row),
        out_shape=jax.ShapeDtypeStruct((t, D_MODEL), F32),
        scratch_shapes=[pltpu.VMEM((TM_MEM, MEM_INNER), BF16)],
        compiler_params=pltpu.CompilerParams(
            dimension_semantics=("arbitrary",), vmem_limit_bytes=32 * MIB),
        name="mem_attn",
    )(x2, gpre, wq, km, vm, wo, gpost)


def _mlp_kernel(x_ref, gpre_ref, wup_ref, wdn_ref, gpost_ref, o_ref):
    part_rows = TM_MLP // MLP_SPLIT
    for part in range(MLP_SPLIT):
        rows = slice(part * part_rows, (part + 1) * part_rows)
        x = x_ref[rows, :]
        h = _rms(x, gpre_ref[...]).astype(BF16)
        y = jnp.zeros((part_rows, D_MODEL), F32)
        for c in range(D_FF // FF_CHUNK):
            cols = slice(c * FF_CHUNK, (c + 1) * FF_CHUNK)
            a = jnp.maximum(jnp.dot(h, wup_ref[:, cols], preferred_element_type=F32), 0.0)
            y = y + jnp.dot((a * a).astype(BF16), wdn_ref[cols, :], preferred_element_type=F32)
        o_ref[rows, :] = x + _rms(y, gpost_ref[...])


def _mlp(x2, gpre, wup, wdn, gpost):
    t = x2.shape[0]
    const = lambda i: (0, 0)
    row = lambda i: (i, 0)
    return pl.pallas_call(
        _mlp_kernel,
        grid=(t // TM_MLP,),
        in_specs=[
            pl.BlockSpec((TM_MLP, D_MODEL), row),
            pl.BlockSpec((1, D_MODEL), const),
            pl.BlockSpec((D_MODEL, D_FF), const, pipeline_mode=pl.Buffered(1)),
            pl.BlockSpec((D_FF, D_MODEL), const, pipeline_mode=pl.Buffered(1)),
            pl.BlockSpec((1, D_MODEL), const),
        ],
        out_specs=pl.BlockSpec((TM_MLP, D_MODEL), row),
        out_shape=jax.ShapeDtypeStruct((t, D_MODEL), F32),
        compiler_params=pltpu.CompilerParams(
            dimension_semantics=("arbitrary",), vmem_limit_bytes=56 * MIB),
        name="mlp",
    )(x2, gpre, wup, wdn, gpost)


def kernel(x, mem, norm_mix_pre, norm_mix_post, w_in, b_forget, conv_w, conv_b, conv_ln_g, conv_ln_b, w_out, norm_mem_pre, norm_mem_post, norm_memkv, w_mq, w_mk, w_mv, w_mo, norm_mlp_pre, norm_mlp_post, w_up, w_down):
    batch, seq, d = x.shape
    depth = w_in.shape[0]
    assert d == D_MODEL and mem.shape[1] == N_MEM
    assert all(seq % tile == 0 for tile in (TM_IN * IN_SPLIT, TM_OUT, TQ * Q_PER_STEP, TM_MEM, TM_MLP))
    x2 = x.reshape(batch * seq, d)
    mem2 = mem.reshape(batch * N_MEM, d)
    tri = (jnp.arange(TM_IN)[:, None] >= jnp.arange(TM_IN)[None, :]).astype(BF16)
    row = lambda a: a.reshape(1, -1)
    f_pad = LANES - FOX_HEADS
    for l in range(depth):
        wag = w_in[l, :, :2 * CONV_CH].astype(BF16)
        wqkv = jnp.pad(w_in[l, :, 2 * CONV_CH:], ((0, 0), (0, f_pad))).astype(BF16)
        bfp = jnp.pad(b_forget[l], (0, f_pad)).reshape(1, LANES)
        u, q, k, v, ccol, crow = _mixer_in(x2, row(norm_mix_pre[l]), wag, wqkv, bfp, tri, seq)
        att = _fox_attn(q, k, v, ccol, crow, batch, seq)
        cw8 = jnp.broadcast_to(conv_w[l][:, None, :], (CONV_WIDTH, SUBLANES, CONV_CH))
        x2 = _mixer_out(x2, u, att, cw8, row(conv_b[l]), row(conv_ln_g[l]),
                        row(conv_ln_b[l]), w_out[l].astype(BF16), row(norm_mix_post[l]), seq)
        km, vm = _mem_kv(mem2, row(norm_memkv[l]), w_mk[l].astype(BF16), w_mv[l].astype(BF16), batch)
        x2 = _mem_attn(x2, row(norm_mem_pre[l]), w_mq[l].astype(BF16), km, vm,
                       w_mo[l].astype(BF16), row(norm_mem_post[l]), seq)
        x2 = _mlp(x2, row(norm_mlp_pre[l]), w_up[l].astype(BF16), w_down[l].astype(BF16),
                  row(norm_mlp_post[l]))
    return x2.reshape(batch, seq, d)
```

```python
import functools

import jax
import jax.numpy as jnp
from jax import lax
from jax.experimental import pallas as pl
from jax.experimental.pallas import tpu as pltpu

D_MODEL = 1024
CONV_CH = 512
CONV_WIDTH = 31
FOX_HEADS = 8
FOX_HEAD_DIM = 64
FOX_WIDTH = FOX_HEADS * FOX_HEAD_DIM
N_MEM = 256
MEM_HEADS = 4
MEM_HEAD_DIM = 128
MEM_INNER = MEM_HEADS * MEM_HEAD_DIM
D_FF = 4 * D_MODEL
EPS = 1e-6
NEG_INF = -1e30
LOG2E = 1.4426950408889634

LANES = 128
SUBLANES = 8
MIB = 1024 * 1024

TM_IN = 512
IN_SPLIT = 2
TM_OUT = 1024
OUT_SPLIT = 4
CONV_HALO = 32
CONV_ROWS = 16
TQ = 256
TK = 256
Q_PER_STEP = 2
HEAD_GROUP = 4
GROUP_W = HEAD_GROUP * FOX_HEAD_DIM
V_SLABS_W = FOX_HEADS * LANES
KV_UNROLLS = (8, 4, 2, 1)
TM_MEM = 1024
MEM_SPLIT = 2
TM_MLP = 1024
MLP_SPLIT = 2
FF_CHUNK = 1024

BF16 = jnp.bfloat16
F32 = jnp.float32


def _rms(x, gain):
    return x * lax.rsqrt(jnp.mean(x * x, axis=-1, keepdims=True) + EPS) * gain


def _sigmoid(x):
    return 1.0 / (1.0 + jnp.exp(-x))


def _log_sigmoid(x):
    return -(jnp.maximum(-x, 0.0) + jnp.log1p(jnp.exp(-jnp.abs(x))))


def _split3(x):
    hi = x.astype(BF16)
    r1 = x - hi.astype(F32)
    mid = r1.astype(BF16)
    lo = (r1 - mid.astype(F32)).astype(BF16)
    return hi, mid, lo


def _mixer_in_kernel(tiles_per_seq, x_ref, g_ref, wag_ref, wqkv_ref, bf_ref, tri_ref,
                     u_ref, q_ref, k_ref, v_ref, ccol_ref, crow_ref, carry_ref):
    i = pl.program_id(0)

    @pl.when(i % tiles_per_seq == 0)
    def _():
        carry_ref[...] = jnp.zeros_like(carry_ref)

    lane = lax.broadcasted_iota(jnp.int32, (TM_IN, LANES), 1)
    carry = carry_ref[0:1, :]
    for part in range(IN_SPLIT):
        rows = slice(part * TM_IN, (part + 1) * TM_IN)
        h = _rms(x_ref[rows, :], g_ref[...]).astype(BF16)
        ag = jnp.dot(h, wag_ref[...], preferred_element_type=F32)
        u_ref[rows, :] = ag[:, :CONV_CH] * _sigmoid(ag[:, CONV_CH:])
        qkv = jnp.dot(h, wqkv_ref[...], preferred_element_type=F32)
        q_ref[rows, :] = (qkv[:, :FOX_WIDTH] * (FOX_HEAD_DIM ** -0.5 * LOG2E)).astype(BF16)
        k_ref[rows, :] = qkv[:, FOX_WIDTH:2 * FOX_WIDTH].astype(BF16)
        for hd in range(FOX_HEADS):
            pair = qkv[:, 2 * FOX_WIDTH + (hd // 2) * LANES:2 * FOX_WIDTH + (hd // 2 + 1) * LANES]
            own = (lane < FOX_HEAD_DIM) if hd % 2 == 0 else (lane >= FOX_HEAD_DIM)
            v_ref[rows, hd * LANES:(hd + 1) * LANES] = jnp.where(own, pair, 1.0).astype(BF16)
        f = qkv[:, 3 * FOX_WIDTH:] + bf_ref[...]
        logf = _log_sigmoid(f)
        tri = tri_ref[...]
        parts = jnp.dot(tri, jnp.concatenate(_split3(logf), axis=1), preferred_element_type=F32)
        cum = (parts[:, :LANES] + parts[:, LANES:2 * LANES] + parts[:, 2 * LANES:]) + carry
        carry = cum[TM_IN - 1:TM_IN, :]
        cum2 = cum * LOG2E
        ccol_ref[rows, :] = cum2
        crow_ref[:, rows] = cum2.T[:FOX_HEADS, :]
    carry_ref[...] = jnp.broadcast_to(carry, carry_ref.shape)


def _mixer_in(x2, gain, wag, wqkv, bfp, tri, seq):
    t = x2.shape[0]
    const = lambda i: (0, 0)
    row = lambda i: (i, 0)
    tm = TM_IN * IN_SPLIT
    return pl.pallas_call(
        functools.partial(_mixer_in_kernel, seq // tm),
        grid=(t // tm,),
        in_specs=[
            pl.BlockSpec((tm, D_MODEL), row),
            pl.BlockSpec((1, D_MODEL), const),
            pl.BlockSpec((D_MODEL, 2 * CONV_CH), const),
            pl.BlockSpec((D_MODEL, 3 * FOX_WIDTH + LANES), const),
            pl.BlockSpec((1, LANES), const),
            pl.BlockSpec((TM_IN, TM_IN), const),
        ],
        out_specs=[
            pl.BlockSpec((tm, CONV_CH), row),
            pl.BlockSpec((tm, FOX_WIDTH), row),
            pl.BlockSpec((tm, FOX_WIDTH), row),
            pl.BlockSpec((tm, V_SLABS_W), row),
            pl.BlockSpec((tm, LANES), row),
            pl.BlockSpec((FOX_HEADS, tm), lambda i: (0, i)),
        ],
        out_shape=[
            jax.ShapeDtypeStruct((t, CONV_CH), F32),
            jax.ShapeDtypeStruct((t, FOX_WIDTH), BF16),
            jax.ShapeDtypeStruct((t, FOX_WIDTH), BF16),
            jax.ShapeDtypeStruct((t, V_SLABS_W), BF16),
            jax.ShapeDtypeStruct((t, LANES), F32),
            jax.ShapeDtypeStruct((FOX_HEADS, t), F32),
        ],
        scratch_shapes=[pltpu.VMEM((SUBLANES, LANES), F32)],
        compiler_params=pltpu.CompilerParams(
            dimension_semantics=("arbitrary",), vmem_limit_bytes=48 * MIB),
        name="mixer_in",
    )(x2, gain, wag, wqkv, bfp, tri)


def _fox_attn_kernel(q_ref, k_ref, v_ref, ccol_ref, crow_ref, hmask_ref, spread_ref, o_ref,
                     qcat_ref, s_ref, mx_ref, r_ref, acc_ref):
    def one_block(qb, carry):
        rows = pl.ds(pl.multiple_of(qb * TQ, TQ), TQ)
        _attend_block(pl.program_id(1) * Q_PER_STEP + qb, q_ref.at[rows], k_ref, v_ref,
                      ccol_ref.at[rows], crow_ref, hmask_ref, spread_ref, o_ref.at[rows],
                      qcat_ref, s_ref, mx_ref, r_ref, acc_ref)
        return carry

    lax.fori_loop(0, Q_PER_STEP, one_block, 0)


def _attend_block(i, q_ref, k_ref, v_ref, ccol_ref, crow_ref, hmask_ref, spread_ref, o_ref,
                  qcat_ref, s_ref, mx_ref, r_ref, acc_ref):
    n_groups = FOX_HEADS // HEAD_GROUP
    for g in range(n_groups):
        qg = q_ref[:, g * GROUP_W:(g + 1) * GROUP_W]
        for h in range(HEAD_GROUP):
            qcat_ref[g, h * TQ:(h + 1) * TQ, :] = qg * hmask_ref[h:h + 1, :]
    mx_ref[...] = jnp.full(mx_ref.shape, NEG_INF, F32)

    def scores(j0, n_blocks):
        for u in range(n_blocks):
            ks = pl.multiple_of((j0 + u) * TK, TK)
            for g in range(n_groups):
                kj = k_ref[pl.ds(ks, TK), g * GROUP_W:(g + 1) * GROUP_W]
                s = lax.dot_general(qcat_ref[g], kj, (((1,), (1,)), ((), ())),
                                    preferred_element_type=F32)
                for h in range(HEAD_GROUP):
                    hg = g * HEAD_GROUP + h
                    t = s[h * TQ:(h + 1) * TQ, :] - crow_ref[hg:hg + 1, pl.ds(ks, TK)]
                    s_ref[hg, :, pl.ds(ks, TK)] = t
                    m = mx_ref[hg]
                    for c in range(TK // LANES):
                        m = jnp.maximum(m, t[:, c * LANES:(c + 1) * LANES])
                    mx_ref[hg] = m

    def loop_blocks(n, step):
        start = 0
        for width in KV_UNROLLS:
            trips = (n - start) // width

            def body(jj, carry, width=width, start=start):
                step(start + jj * width, width)
                return carry

            lax.fori_loop(0, trips, body, 0)
            start = start + trips * width

    loop_blocks(i, scores)

    kd = pl.multiple_of(i * TK, TK)
    causal = (lax.broadcasted_iota(jnp.int32, (TQ, TK), 1)
              <= lax.broadcasted_iota(jnp.int32, (TQ, TK), 0))
    cq_rep = jnp.dot(ccol_ref[...].astype(BF16), spread_ref[...], preferred_element_type=F32)
    for g in range(n_groups):
        s = lax.dot_general(qcat_ref[g], k_ref[pl.ds(kd, TK), g * GROUP_W:(g + 1) * GROUP_W],
                            (((1,), (1,)), ((), ())), preferred_element_type=F32)
        for h in range(HEAD_GROUP):
            hg = g * HEAD_GROUP + h
            t = jnp.where(causal, s[h * TQ:(h + 1) * TQ, :] - crow_ref[hg:hg + 1, pl.ds(kd, TK)],
                          NEG_INF)
            m_lane = mx_ref[hg]
            for c in range(TK // LANES):
                m_lane = jnp.maximum(m_lane, t[:, c * LANES:(c + 1) * LANES])
            cq = cq_rep[:, hg * LANES:(hg + 1) * LANES]
            m = jnp.max(m_lane, axis=-1, keepdims=True) + cq
            r = cq - m
            r_ref[hg] = r
            ps = [jnp.exp2(t[:, c * LANES:(c + 1) * LANES] + r).astype(BF16)
                  for c in range(TK // LANES)]
            acc_ref[hg] = jnp.dot(jnp.concatenate(ps, axis=1),
                                  v_ref[pl.ds(kd, TK), hg * LANES:(hg + 1) * LANES],
                                  preferred_element_type=F32)

    def pv(j0, n_blocks):
        ks = pl.multiple_of(j0 * TK, TK)
        for hg in range(FOX_HEADS):
            r = r_ref[hg]
            acc = acc_ref[hg]
            for b in range(n_blocks):
                ps = [jnp.exp2(s_ref[hg, :, pl.ds(ks + b * TK + c * LANES, LANES)] + r).astype(BF16)
                      for c in range(TK // LANES)]
                acc = acc + jnp.dot(jnp.concatenate(ps, axis=1),
                                    v_ref[pl.ds(ks + b * TK, TK), hg * LANES:(hg + 1) * LANES],
                                    preferred_element_type=F32)
            acc_ref[hg] = acc

    loop_blocks(i, pv)

    first_half = lax.broadcasted_iota(jnp.int32, (TQ, LANES), 1) < FOX_HEAD_DIM
    for pair in range(FOX_HEADS // 2):
        even = acc_ref[2 * pair]
        odd = acc_ref[2 * pair + 1]
        num = jnp.where(first_half, even, odd)
        den = pltpu.roll(jnp.where(first_half, odd, even), FOX_HEAD_DIM, axis=1)
        o_ref[:, pair * LANES:(pair + 1) * LANES] = (num / den).astype(BF16)


def _fox_attn(q, k, v, ccol, crow, batch, seq):
    t = q.shape[0]
    step_rows = TQ * Q_PER_STEP
    nq = seq // step_rows
    hmask = (jnp.arange(GROUP_W)[None, :] // FOX_HEAD_DIM == jnp.arange(HEAD_GROUP)[:, None]).astype(BF16)
    spread = (jnp.arange(V_SLABS_W)[None, :] // LANES == jnp.arange(LANES)[:, None]).astype(BF16)
    qrow = lambda b, i: (b * nq + i, 0)
    whole_seq = lambda b, i: (b, 0)
    return pl.pallas_call(
        _fox_attn_kernel,
        grid=(batch, nq),
        in_specs=[
            pl.BlockSpec((step_rows, FOX_WIDTH), qrow),
            pl.BlockSpec((seq, FOX_WIDTH), whole_seq),
            pl.BlockSpec((seq, V_SLABS_W), whole_seq, pipeline_mode=pl.Buffered(1)),
            pl.BlockSpec((step_rows, LANES), qrow),
            pl.BlockSpec((FOX_HEADS, seq), lambda b, i: (0, b)),
            pl.BlockSpec((HEAD_GROUP, GROUP_W), lambda b, i: (0, 0)),
            pl.BlockSpec((LANES, V_SLABS_W), lambda b, i: (0, 0)),
        ],
        out_specs=pl.BlockSpec((step_rows, FOX_WIDTH), qrow),
        out_shape=jax.ShapeDtypeStruct((t, FOX_WIDTH), BF16),
        scratch_shapes=[
            pltpu.VMEM((FOX_HEADS // HEAD_GROUP, HEAD_GROUP * TQ, GROUP_W), BF16),
            pltpu.VMEM((FOX_HEADS, TQ, seq), F32),
            pltpu.VMEM((FOX_HEADS, TQ, LANES), F32),
            pltpu.VMEM((FOX_HEADS, TQ, LANES), F32),
            pltpu.VMEM((FOX_HEADS, TQ, LANES), F32),
        ],
        compiler_params=pltpu.CompilerParams(
            dimension_semantics=("arbitrary", "arbitrary"), vmem_limit_bytes=58 * MIB),
        name="fox_attn",
    )(q, k, v, ccol, crow, hmask, spread)


def _mixer_out_kernel(tiles_per_seq, x_ref, u_ref, uprev_ref, att_ref, cw_ref, cb_ref, lng_ref,
                      lnb_ref, wo_ref, gpost_ref, o_ref, ubuf_ref, cat_ref):
    i = pl.program_id(0)
    first = (i % tiles_per_seq == 0)
    ubuf_ref[0, 0:CONV_HALO, :] = jnp.where(first, 0.0, uprev_ref[...])
    ubuf_ref[0, CONV_HALO:, :] = u_ref[...]
    shifted_rows = CONV_HALO + TM_OUT - SUBLANES
    for s in range(1, SUBLANES):
        ubuf_ref[s, 0:shifted_rows, :] = ubuf_ref[0, s:s + shifted_rows, :]
    base = CONV_HALO - (CONV_WIDTH - 1)
    cat_ref[:, CONV_CH:] = att_ref[...]
    part_rows = TM_OUT // OUT_SPLIT
    for part in range(OUT_SPLIT):
        for c in range(part_rows // CONV_ROWS):
            r0 = part * part_rows + c * CONV_ROWS
            accs = [jnp.broadcast_to(cb_ref[...], (SUBLANES, CONV_CH))] * (CONV_ROWS // SUBLANES)
            for kk in range(CONV_WIDTH):
                s = (base + kk) % SUBLANES
                a = r0 + base + kk - s
                w8 = cw_ref[kk]
                accs = [acc + w8 * ubuf_ref[s, a + g * SUBLANES:a + (g + 1) * SUBLANES, :]
                        for g, acc in enumerate(accs)]
            acc = jnp.concatenate(accs, axis=0)
            mu = jnp.mean(acc, axis=-1, keepdims=True)
            xc = acc - mu
            y = xc * lax.rsqrt(jnp.mean(xc * xc, axis=-1, keepdims=True) + EPS)
            y = y * lng_ref[...] + lnb_ref[...]
            cat_ref[r0:r0 + CONV_ROWS, 0:CONV_CH] = (y * _sigmoid(y)).astype(BF16)
        rows = slice(part * part_rows, (part + 1) * part_rows)
        y = jnp.dot(cat_ref[rows, :], wo_ref[...], preferred_element_type=F32)
        o_ref[rows, :] = x_ref[rows, :] + _rms(y, gpost_ref[...])


def _mixer_out(x2, u, att, cw, cb, lng, lnb, wo, gpost, seq):
    t = x2.shape[0]
    const = lambda i: (0, 0)
    row = lambda i: (i, 0)
    halo_blocks = TM_OUT // CONV_HALO
    return pl.pallas_call(
        functools.partial(_mixer_out_kernel, seq // TM_OUT),
        grid=(t // TM_OUT,),
        in_specs=[
            pl.BlockSpec((TM_OUT, D_MODEL), row),
            pl.BlockSpec((TM_OUT, CONV_CH), row),
            pl.BlockSpec((CONV_HALO, CONV_CH), lambda i: (jnp.maximum(i * halo_blocks - 1, 0), 0)),
            pl.BlockSpec((TM_OUT, FOX_WIDTH), row),
            pl.BlockSpec((CONV_WIDTH, SUBLANES, CONV_CH), lambda i: (0, 0, 0)),
            pl.BlockSpec((1, CONV_CH), const),
            pl.BlockSpec((1, CONV_CH), const),
            pl.BlockSpec((1, CONV_CH), const),
            pl.BlockSpec((CONV_CH + FOX_WIDTH, D_MODEL), const),
            pl.BlockSpec((1, D_MODEL), const),
        ],
        out_specs=pl.BlockSpec((TM_OUT, D_MODEL), row),
        out_shape=jax.ShapeDtypeStruct((t, D_MODEL), F32),
        scratch_shapes=[
            pltpu.VMEM((SUBLANES, CONV_HALO + TM_OUT, CONV_CH), F32),
            pltpu.VMEM((TM_OUT, CONV_CH + FOX_WIDTH), BF16),
        ],
        compiler_params=pltpu.CompilerParams(
            dimension_semantics=("arbitrary",), vmem_limit_bytes=56 * MIB),
        name="mixer_out",
    )(x2, u, u, att, cw, cb, lng, lnb, wo, gpost)


def _mem_kv_kernel(mem_ref, g_ref, wk_ref, wv_ref, k_ref, v_ref):
    mn = _rms(mem_ref[...], g_ref[...]).astype(BF16)
    k_ref[...] = jnp.dot(mn, wk_ref[...], preferred_element_type=F32).astype(BF16)
    v_ref[...] = jnp.dot(mn, wv_ref[...], preferred_element_type=F32).astype(BF16)


def _mem_kv(mem2, gain, wk, wv, batch):
    const = lambda b: (0, 0)
    row = lambda b: (b, 0)
    return pl.pallas_call(
        _mem_kv_kernel,
        grid=(batch,),
        in_specs=[
            pl.BlockSpec((N_MEM, D_MODEL), row),
            pl.BlockSpec((1, D_MODEL), const),
            pl.BlockSpec((D_MODEL, MEM_INNER), const),
            pl.BlockSpec((D_MODEL, MEM_INNER), const),
        ],
        out_specs=[pl.BlockSpec((N_MEM, MEM_INNER), row), pl.BlockSpec((N_MEM, MEM_INNER), row)],
        out_shape=[jax.ShapeDtypeStruct((batch * N_MEM, MEM_INNER), BF16)] * 2,
        compiler_params=pltpu.CompilerParams(dimension_semantics=("arbitrary",)),
        name="mem_kv",
    )(mem2, gain, wk, wv)


def _mem_attn_kernel(x_ref, gpre_ref, wq_ref, km_ref, vm_ref, wo_ref, gpost_ref, o_ref, cat_ref):
    part_rows = TM_MEM // MEM_SPLIT
    for part in range(MEM_SPLIT):
        rows = slice(part * part_rows, (part + 1) * part_rows)
        x = x_ref[rows, :]
        h = _rms(x, gpre_ref[...]).astype(BF16)
        q = jnp.dot(h, wq_ref[...], preferred_element_type=F32)
        for hd in range(MEM_HEADS):
            lanes = slice(hd * MEM_HEAD_DIM, (hd + 1) * MEM_HEAD_DIM)
            s = lax.dot_general(q[:, lanes].astype(BF16), km_ref[:, lanes], (((1,), (1,)), ((), ())),
                                preferred_element_type=F32) * (MEM_HEAD_DIM ** -0.5)
            p = jnp.exp(s - jnp.max(s, axis=-1, keepdims=True))
            l = jnp.sum(p, axis=-1, keepdims=True)
            o = jnp.dot(p.astype(BF16), vm_ref[:, lanes], preferred_element_type=F32)
            cat_ref[rows, lanes] = (o / l).astype(BF16)
        y = jnp.dot(cat_ref[rows, :], wo_ref[...], preferred_element_type=F32)
        o_ref[rows, :] = x + _rms(y, gpost_ref[...])


def _mem_attn(x2, gpre, wq, km, vm, wo, gpost, seq):
    t = x2.shape[0]
    tiles_per_seq = seq // TM_MEM
    const = lambda i: (0, 0)
    row = lambda i: (i, 0)
    per_batch = lambda i: (i // tiles_per_seq, 0)
    return pl.pallas_call(
        _mem_attn_kernel,
        grid=(t // TM_MEM,),
        in_specs=[
            pl.BlockSpec((TM_MEM, D_MODEL), row),
            pl.BlockSpec((1, D_MODEL), const),
            pl.BlockSpec((D_MODEL, MEM_INNER), const),
            pl.BlockSpec((N_MEM, MEM_INNER), per_batch),
            pl.BlockSpec((N_MEM, MEM_INNER), per_batch),
            pl.BlockSpec((MEM_INNER, D_MODEL), const),
            pl.BlockSpec((1, D_MODEL), const),
        ],
        out_specs=pl.BlockSpec((TM_MEM, D_MODEL), row),
        out_shape=jax.ShapeDtypeStruct((t, D_MODEL), F32),
        scratch_shapes=[pltpu.VMEM((TM_MEM, MEM_INNER), BF16)],
        compiler_params=pltpu.CompilerParams(
            dimension_semantics=("arbitrary",), vmem_limit_bytes=32 * MIB),
        name="mem_attn",
    )(x2, gpre, wq, km, vm, wo, gpost)


def _mlp_kernel(x_ref, gpre_ref, wup_ref, wdn_ref, gpost_ref, o_ref):
    part_rows = TM_MLP // MLP_SPLIT
    for part in range(MLP_SPLIT):
        rows = slice(part * part_rows, (part + 1) * part_rows)
        x = x_ref[rows, :]
        h = _rms(x, gpre_ref[...]).astype(BF16)
        y = jnp.zeros((part_rows, D_MODEL), F32)
        for c in range(D_FF // FF_CHUNK):
            cols = slice(c * FF_CHUNK, (c + 1) * FF_CHUNK)
            a = jnp.maximum(jnp.dot(h, wup_ref[:, cols], preferred_element_type=F32), 0.0)
            y = y + jnp.dot((a * a).astype(BF16), wdn_ref[cols, :], preferred_element_type=F32)
        o_ref[rows, :] = x + _rms(y, gpost_ref[...])


def _mlp(x2, gpre, wup, wdn, gpost):
    t = x2.shape[0]
    const = lambda i: (0, 0)
    row = lambda i: (i, 0)
    return pl.pallas_call(
        _mlp_kernel,
        grid=(t // TM_MLP,),
        in_specs=[
            pl.BlockSpec((TM_MLP, D_MODEL), row),
            pl.BlockSpec((1, D_MODEL), const),
            pl.BlockSpec((D_MODEL, D_FF), const, pipeline_mode=pl.Buffered(1)),
            pl.BlockSpec((D_FF, D_MODEL), const, pipeline_mode=pl.Buffered(1)),
            pl.BlockSpec((1, D_MODEL), const),
        ],
        out_specs=pl.BlockSpec((TM_MLP, D_MODEL), row),
        out_shape=jax.ShapeDtypeStruct((t, D_MODEL), F32),
        compiler_params=pltpu.CompilerParams(
            dimension_semantics=("arbitrary",), vmem_limit_bytes=56 * MIB),
        name="mlp",
    )(x2, gpre, wup, wdn, gpost)


def _mem_mlp_kernel(x_ref, gm_pre_ref, wq_ref, km_ref, vm_ref, wo_ref, gm_post_ref,
                    gf_pre_ref, wup_ref, wdn_ref, gf_post_ref, o_ref, cat_ref):
    part_rows = TM_MLP // MLP_SPLIT
    for part in range(MLP_SPLIT):
        rows = slice(part * part_rows, (part + 1) * part_rows)
        x = x_ref[rows, :]
        h = _rms(x, gm_pre_ref[...]).astype(BF16)
        q = jnp.dot(h, wq_ref[...], preferred_element_type=F32)
        for hd in range(MEM_HEADS):
            lanes = slice(hd * MEM_HEAD_DIM, (hd + 1) * MEM_HEAD_DIM)
            s = lax.dot_general(q[:, lanes].astype(BF16), km_ref[:, lanes], (((1,), (1,)), ((), ())),
                                preferred_element_type=F32) * (MEM_HEAD_DIM ** -0.5)
            p = jnp.exp(s - jnp.max(s, axis=-1, keepdims=True))
            l = jnp.sum(p, axis=-1, keepdims=True)
            o = jnp.dot(p.astype(BF16), vm_ref[:, lanes], preferred_element_type=F32)
            cat_ref[rows, lanes] = (o / l).astype(BF16)
        y = jnp.dot(cat_ref[rows, :], wo_ref[...], preferred_element_type=F32)
        x = x + _rms(y, gm_post_ref[...])
        h = _rms(x, gf_pre_ref[...]).astype(BF16)
        y = jnp.zeros((part_rows, D_MODEL), F32)
        for c in range(D_FF // FF_CHUNK):
            cols = slice(c * FF_CHUNK, (c + 1) * FF_CHUNK)
            a = jnp.maximum(jnp.dot(h, wup_ref[:, cols], preferred_element_type=F32), 0.0)
            y = y + jnp.dot((a * a).astype(BF16), wdn_ref[cols, :], preferred_element_type=F32)
        o_ref[rows, :] = x + _rms(y, gf_post_ref[...])


def _mem_mlp(x2, gm_pre, wq, km, vm, wo, gm_post, gf_pre, wup, wdn, gf_post, seq):
    t = x2.shape[0]
    tiles_per_seq = seq // TM_MLP
    const = lambda i: (0, 0)
    row = lambda i: (i, 0)
    per_batch = lambda i: (i // tiles_per_seq, 0)
    once = pl.Buffered(1)
    return pl.pallas_call(
        _mem_mlp_kernel,
        grid=(t // TM_MLP,),
        in_specs=[
            pl.BlockSpec((TM_MLP, D_MODEL), row),
            pl.BlockSpec((1, D_MODEL), const),
            pl.BlockSpec((D_MODEL, MEM_INNER), const, pipeline_mode=once),
            pl.BlockSpec((N_MEM, MEM_INNER), per_batch),
            pl.BlockSpec((N_MEM, MEM_INNER), per_batch),
            pl.BlockSpec((MEM_INNER, D_MODEL), const, pipeline_mode=once),
            pl.BlockSpec((1, D_MODEL), const),
            pl.BlockSpec((1, D_MODEL), const),
            pl.BlockSpec((D_MODEL, D_FF), const, pipeline_mode=once),
            pl.BlockSpec((D_FF, D_MODEL), const, pipeline_mode=once),
            pl.BlockSpec((1, D_MODEL), const),
        ],
        out_specs=pl.BlockSpec((TM_MLP, D_MODEL), row),
        out_shape=jax.ShapeDtypeStruct((t, D_MODEL), F32),
        scratch_shapes=[pltpu.VMEM((TM_MLP, MEM_INNER), BF16)],
        compiler_params=pltpu.CompilerParams(
            dimension_semantics=("arbitrary",), vmem_limit_bytes=56 * MIB),
        name="mem_mlp",
    )(x2, gm_pre, wq, km, vm, wo, gm_post, gf_pre, wup, wdn, gf_post)


def kernel(x, mem, norm_mix_pre, norm_mix_post, w_in, b_forget, conv_w, conv_b, conv_ln_g, conv_ln_b, w_out, norm_mem_pre, norm_mem_post, norm_memkv, w_mq, w_mk, w_mv, w_mo, norm_mlp_pre, norm_mlp_post, w_up, w_down):
    batch, seq, d = x.shape
    depth = w_in.shape[0]
    assert d == D_MODEL and mem.shape[1] == N_MEM
    assert all(seq % tile == 0 for tile in (TM_IN * IN_SPLIT, TM_OUT, TQ * Q_PER_STEP, TM_MEM, TM_MLP))
    x2 = x.reshape(batch * seq, d)
    mem2 = mem.reshape(batch * N_MEM, d)
    tri = (jnp.arange(TM_IN)[:, None] >= jnp.arange(TM_IN)[None, :]).astype(BF16)
    row = lambda a: a.reshape(1, -1)
    f_pad = LANES - FOX_HEADS
    for l in range(depth):
        wag = w_in[l, :, :2 * CONV_CH].astype(BF16)
        wqkv = jnp.pad(w_in[l, :, 2 * CONV_CH:], ((0, 0), (0, f_pad))).astype(BF16)
        bfp = jnp.pad(b_forget[l], (0, f_pad)).reshape(1, LANES)
        u, q, k, v, ccol, crow = _mixer_in(x2, row(norm_mix_pre[l]), wag, wqkv, bfp, tri, seq)
        att = _fox_attn(q, k, v, ccol, crow, batch, seq)
        cw8 = jnp.broadcast_to(conv_w[l][:, None, :], (CONV_WIDTH, SUBLANES, CONV_CH))
        x2 = _mixer_out(x2, u, att, cw8, row(conv_b[l]), row(conv_ln_g[l]),
                        row(conv_ln_b[l]), w_out[l].astype(BF16), row(norm_mix_post[l]), seq)
        km, vm = _mem_kv(mem2, row(norm_memkv[l]), w_mk[l].astype(BF16), w_mv[l].astype(BF16), batch)
        x2 = _mem_mlp(x2, row(norm_mem_pre[l]), w_mq[l].astype(BF16), km, vm,
                      w_mo[l].astype(BF16), row(norm_mem_post[l]),
                      row(norm_mlp_pre[l]), w_up[l].astype(BF16), w_down[l].astype(BF16),
                      row(norm_mlp_post[l]), seq)
    return x2.reshape(batch, seq, d)
```
